```python
import jax, jax.numpy as jnp
from jax import lax
import numpy as np

D_MODEL = 1024
BATCH = 16
SEQ = 2048
DEPTH = 2

EPS = 1e-6
N_BRANCH = 3
BRANCH_WIDTH = D_MODEL // 2
GLA_HEADS = 4
GLA_DV = BRANCH_WIDTH // GLA_HEADS
GLA_DK = GLA_DV // 2
GLA_RANK = 16
GLA_TAU = 16.0
GLA_CHUNK = 16
GLA_QK = GLA_HEADS * GLA_DK
GLA_V = GLA_HEADS * GLA_DV
SC_WIDTH = BRANCH_WIDTH
CONV_K = 3
DIL_PATTERNS = ((128, 1), (512, 4), (2048, 16))
DIL_GROUPS = len(DIL_PATTERNS)
DIL_HEADS = 4
DIL_HD = BRANCH_WIDTH // DIL_HEADS
DIL_W = DIL_GROUPS * DIL_HEADS * DIL_HD
DIL_BLOCK = 128
ROPE_THETA = 10000.0
D_FF = 2816

IN_SIZES = (GLA_QK, GLA_QK, GLA_V, GLA_V, GLA_RANK,
            SC_WIDTH, SC_WIDTH, SC_WIDTH,
            DIL_W, DIL_W, DIL_W)
IN_COLS = sum(IN_SIZES)

kernel_name = "hybrid_gated_gla_shortconv_dilated_attn"


def rmsnorm(x, g):
    xf = x.astype(jnp.float32)
    r = lax.rsqrt(jnp.mean(xf * xf, axis=-1, keepdims=True) + EPS)
    return (xf * r).astype(x.dtype) * g


def causal_dwconv(u, w):
    K = w.shape[0]
    S = u.shape[1]
    up = jnp.pad(u, ((0, 0), (K - 1, 0), (0, 0)))
    y = w[0] * up[:, 0:S]
    for k in range(1, K):
        y = y + w[k] * up[:, k:k + S]
    return y


def rope(x, pos):
    hd = x.shape[-1]
    inv = ROPE_THETA ** (-jnp.arange(0, hd, 2, dtype=jnp.float32) / hd)
    ang = pos.astype(jnp.float32)[..., None] * inv
    cos = jnp.cos(ang)[:, :, None, None, :]
    sin = jnp.sin(ang)[:, :, None, None, :]
    xf = x.astype(jnp.float32)
    x1, x2 = xf[..., :hd // 2], xf[..., hd // 2:]
    return jnp.concatenate([x1 * cos - x2 * sin, x2 * cos + x1 * sin], axis=-1).astype(x.dtype)


def gla_branch(q, k, v, g, a_low, w_a_up, b_a, norm_g):
    B, S, _ = q.shape
    H, C = GLA_HEADS, GLA_CHUNK
    N = S // C
    f32 = jnp.float32
    loga = jax.nn.log_sigmoid((a_low.astype(f32) @ w_a_up.astype(f32)) + b_a.astype(f32)) / GLA_TAU

    def chunks(t, d):
        return t.astype(f32).reshape(B, N, C, H, d).transpose(0, 3, 1, 2, 4)

    qc = chunks(q, GLA_DK) * (GLA_DK ** -0.5)
    kc = chunks(k, GLA_DK)
    vc = chunks(v, GLA_DV)
    lc = jnp.cumsum(chunks(loga, GLA_DK), axis=3)

    causal = jnp.tril(jnp.ones((C, C), dtype=bool))
    rel = lc[..., :, None, :] - lc[..., None, :, :]
    decay = jnp.exp(jnp.where(causal[:, :, None], rel, -jnp.inf))
    scores = jnp.einsum('bhnid,bhnjd,bhnijd->bhnij', qc, kc, decay)
    o_intra = jnp.einsum('bhnij,bhnje->bhnie', scores, vc)

    lend = lc[..., -1:, :]
    q_in = qc * jnp.exp(lc)
    k_in = kc * jnp.exp(lend - lc)
    dec_chunk = jnp.exp(lend[..., 0, :])

    def step(state, xs):
        qn, kn, vn, dn = xs
        o = jnp.einsum('bhid,bhde->bhie', qn, state)
        state = state * dn[..., None] + jnp.einsum('bhjd,bhje->bhde', kn, vn)
        return state, o

    xs = (jnp.moveaxis(q_in, 2, 0), jnp.moveaxis(k_in, 2, 0),
          jnp.moveaxis(vc, 2, 0), jnp.moveaxis(dec_chunk, 2, 0))
    s0 = jnp.zeros((B, H, GLA_DK, GLA_DV), f32)
    _, o_inter = lax.scan(step, s0, xs)
    o = o_intra + jnp.moveaxis(o_inter, 0, 2)
    o = o.transpose(0, 2, 3, 1, 4).reshape(B, S, H, GLA_DV).astype(v.dtype)
    o = rmsnorm(o, norm_g).reshape(B, S, GLA_V)
    return o * jax.nn.silu(g)


def strided_window_attn(q, k, v, span, dil):
    B, S, H, D = q.shape
    L = S // dil
    BLK = DIL_BLOCK
    nb = -(-L // BLK)
    Lp = nb * BLK

    def to_sub(t):
        return t.reshape(B, L, dil, H, D).transpose(0, 2, 3, 1, 4)

    qs, ks, vs = to_sub(q), to_sub(k), to_sub(v)
    qb = jnp.pad(qs, ((0, 0), (0, 0), (0, 0), (0, Lp - L), (0, 0))).reshape(B, dil, H, nb, BLK, D)

    def kv_blocks(t):
        tp = jnp.pad(t, ((0, 0), (0, 0), (0, 0), (BLK, Lp - L), (0, 0)))
        prev = tp[:, :, :, :Lp].reshape(B, dil, H, nb, BLK, D)
        cur = tp[:, :, :, BLK:].reshape(B, dil, H, nb, BLK, D)
        return jnp.concatenate([prev, cur], axis=4)

    kb, vb = kv_blocks(ks), kv_blocks(vs)
    i = jnp.arange(BLK)[:, None]
    m = jnp.arange(2 * BLK)[None, :]
    blk = jnp.arange(nb)[:, None, None]
    dist = i + BLK - m
    key_idx = (blk - 1) * BLK + m
    valid = (dist >= 0) & (dist <= span) & (key_idx >= 0)

    s = jnp.einsum('brhnid,brhnmd->brhnim', qb, kb).astype(jnp.float32) * (D ** -0.5)
    s = jnp.where(valid, s, -jnp.inf)
    mx = jnp.max(s, axis=-1, keepdims=True)
    p = jnp.exp(s - mx)
    den = jnp.sum(p, axis=-1, keepdims=True)
    o = jnp.einsum('brhnim,brhnmd->brhnid', p, vb.astype(jnp.float32)) / den
    lse = (mx + jnp.log(den))[..., 0]

    o = o.reshape(B, dil, H, Lp, D)[:, :, :, :L].transpose(0, 3, 1, 2, 4).reshape(B, S, H, D)
    lse = lse.reshape(B, dil, H, Lp)[:, :, :, :L].transpose(0, 3, 1, 2).reshape(B, S, H)
    return o, lse


def dilated_branch(dq, dk, dv, pos):
    B, S, _ = dq.shape
    shp = (B, S, DIL_GROUPS, DIL_HEADS, DIL_HD)
    q = rope(dq.reshape(shp), pos)
    k = rope(dk.reshape(shp), pos)
    v = dv.reshape(shp)
    outs, lses = [], []
    for gi, (win, dil) in enumerate(DIL_PATTERNS):
        o, lse = strided_window_attn(q[:, :, gi], k[:, :, gi], v[:, :, gi], win // dil, dil)
        outs.append(o)
        lses.append(lse)
    w = jax.nn.softmax(jnp.stack(lses, axis=0), axis=0)
    o = jnp.sum(w[..., None] * jnp.stack(outs, axis=0), axis=0)
    return o.reshape(B, S, DIL_HEADS * DIL_HD).astype(dq.dtype)


def setup_inputs(seed: int = 0) -> dict:
    key = jax.random.key(seed)
    ks = jax.random.split(key, 24)
    f32 = jnp.float32

    def nrm(k, shape, scale):
        return jax.random.normal(k, shape, f32) * scale

    def gain(k, shape):
        return 1.0 + 0.02 * jax.random.normal(k, shape, f32)

    x = jax.random.normal(ks[0], (BATCH, SEQ, D_MODEL), f32)
    offset = jax.random.randint(ks[1], (BATCH, 1), 0, 4096, dtype=jnp.int32)
    positions = (offset + jnp.arange(SEQ, dtype=jnp.int32)[None, :]).astype(jnp.int32)
    return {
        "x": x,
        "positions": positions,
        "w_in": nrm(ks[2], (DEPTH, D_MODEL, IN_COLS), D_MODEL ** -0.5),
        "w_alpha_up": nrm(ks[3], (DEPTH, GLA_RANK, GLA_QK), GLA_RANK ** -0.5),
        "b_alpha": nrm(ks[4], (DEPTH, GLA_QK), 0.1),
        "gla_norm_g": gain(ks[5], (DEPTH, GLA_DV)),
        "sc_conv_w": nrm(ks[6], (DEPTH, CONV_K, SC_WIDTH), CONV_K ** -0.5),
        "w_gate": nrm(ks[7], (DEPTH, D_MODEL, N_BRANCH * D_MODEL), D_MODEL ** -0.5),
        "b_gate": nrm(ks[8], (DEPTH, N_BRANCH * D_MODEL), 0.02),
        "w_branch": nrm(ks[9], (DEPTH, N_BRANCH, BRANCH_WIDTH, D_MODEL), BRANCH_WIDTH ** -0.5),
        "w_mix_out": nrm(ks[10], (DEPTH, D_MODEL, D_MODEL), D_MODEL ** -0.5),
        "pre_mix_g": gain(ks[11], (DEPTH, D_MODEL)),
        "post_mix_g": gain(ks[12], (DEPTH, D_MODEL)),
        "pre_ffn_g": gain(ks[13], (DEPTH, D_MODEL)),
        "post_ffn_g": gain(ks[14], (DEPTH, D_MODEL)),
        "w_ff_gate": nrm(ks[15], (DEPTH, D_MODEL, D_FF), D_MODEL ** -0.5),
        "w_ff_up": nrm(ks[16], (DEPTH, D_MODEL, D_FF), D_MODEL ** -0.5),
        "ff_conv_w": nrm(ks[17], (DEPTH, CONV_K, D_FF), CONV_K ** -0.5),
        "ff_conv_b": nrm(ks[18], (DEPTH, D_FF), 0.02),
        "w_ff_down": nrm(ks[19], (DEPTH, D_FF, D_MODEL), D_FF ** -0.5),
    }


def reference(x, positions, w_in, w_alpha_up, b_alpha, gla_norm_g, sc_conv_w, w_gate, b_gate,
              w_branch, w_mix_out, pre_mix_g, post_mix_g, pre_ffn_g, post_ffn_g,
              w_ff_gate, w_ff_up, ff_conv_w, ff_conv_b, w_ff_down):
    B, S, _ = x.shape
    split_points = np.cumsum(np.array(IN_SIZES))[:-1].tolist()
    for l in range(DEPTH):
        h = rmsnorm(x, pre_mix_g[l])
        z = h @ w_in[l]
        (g_q, g_k, g_v, g_o, a_low, sc_b, sc_c, sc_x,
         d_q, d_k, d_v) = jnp.split(z, split_points, axis=-1)

        y_gla = gla_branch(g_q, g_k, g_v, g_o, a_low, w_alpha_up[l], b_alpha[l], gla_norm_g[l])
        y_sc = sc_b * causal_dwconv(sc_c * sc_x, sc_conv_w[l])
        y_dil = dilated_branch(d_q, d_k, d_v, positions)

        br = jnp.stack([y_gla, y_sc, y_dil], axis=2)
        proj = jnp.einsum('bsgc,gcd->bsgd', br, w_branch[l])
        gates = jax.nn.sigmoid(h @ w_gate[l] + b_gate[l]).reshape(B, S, N_BRANCH, D_MODEL)
        merged = jnp.sum(gates * proj, axis=2)
        x = x + rmsnorm(merged @ w_mix_out[l], post_mix_g[l])

        h = rmsnorm(x, pre_ffn_g[l])
        gt = causal_dwconv(h @ w_ff_gate[l], ff_conv_w[l]) + ff_conv_b[l]
        y = (jax.nn.gelu(gt, approximate=True) * (h @ w_ff_up[l])) @ w_ff_down[l]
        x = x + rmsnorm(y, post_ffn_g[l])
    return x
```

```python
import functools

import jax
import jax.numpy as jnp
import numpy as np
from jax import lax
from jax.experimental import pallas as pl
from jax.experimental.pallas import tpu as pltpu

F32 = jnp.float32
BF16 = jnp.bfloat16

D_MODEL = 1024
EPS = 1e-6
BRANCH_WIDTH = 512
GLA_HEADS = 4
GLA_DV = 128
GLA_DK = 64
GLA_RANK = 16
GLA_TAU = 16.0
GLA_CHUNK = 16
GLA_QK = GLA_HEADS * GLA_DK
GLA_V = GLA_HEADS * GLA_DV
CONV_K = 3
DIL_PATTERNS = ((128, 1), (512, 4), (2048, 16))
DIL_GROUPS = len(DIL_PATTERNS)
DIL_HEADS = 4
DIL_HD = 128
DIL_GW = DIL_HEADS * DIL_HD
DIL_BLOCK = 128
ROPE_THETA = 10000.0
D_FF = 2816

LANES = 128
SUBLANES = 8
MIB = 1024 * 1024

ALOW_PAD = LANES
ZN_COLS = 3 * DIL_GW + 2 * GLA_V + 3 * BRANCH_WIDTH + 2 * GLA_QK + ALOW_PAD
ZB_GV, ZB_GO, ZB_SCB, ZB_SCC, ZB_SCX = 3, 4, 5, 6, 7
ZB_GQ, ZB_GK = 16, 17
ZB_AL = 36
ZG_COLS = 3 * DIL_GW
W_COLS = ZN_COLS + (DIL_GROUPS - 1) * ZG_COLS

NEG_BIG = -1e30


def _cparams(vmem_mib, sem=None):
    return pltpu.CompilerParams(dimension_semantics=sem, vmem_limit_bytes=int(vmem_mib * MIB))


def _const_spec(shape):
    nd = len(shape)
    return pl.BlockSpec(shape, lambda *_: (0,) * nd, pipeline_mode=pl.Buffered(1))


def _rms(xf, g):
    r = lax.rsqrt(jnp.mean(xf * xf, axis=-1, keepdims=True) + EPS)
    return xf * r * g


def _sigmoid(v):
    return 1.0 / (1.0 + jnp.exp(-v))


def _dot(a, b):
    return jnp.dot(a, b, preferred_element_type=F32)


def _rope_kernel(pos_ref, inv_ref, sgn_ref, c0_ref, s0_ref, c1_ref, s1_ref, c2_ref, s2_ref, *, seq):
    ang = pos_ref[...].astype(F32) * inv_ref[...]
    c0_ref[0] = jnp.cos(ang)
    s0_ref[0] = jnp.sin(ang) * sgn_ref[...]
    for (c_ref, s_ref), (_, dil) in zip(((c1_ref, s1_ref), (c2_ref, s2_ref)), DIL_PATTERNS[1:]):
        n = seq // dil
        for r in range(dil):
            c_ref[r] = c0_ref[0, pl.ds(r, n, stride=dil), :]
            s_ref[r] = s0_ref[0, pl.ds(r, n, stride=dil), :]


def _rope_tables(positions):
    batch, seq = positions.shape
    inv = ROPE_THETA ** (-jnp.arange(0, DIL_HD, 2, dtype=F32) / DIL_HD)
    inv2 = jnp.concatenate([inv, inv]).reshape(1, DIL_HD)
    half = DIL_HD // 2
    sgn = jnp.concatenate([-jnp.ones((half,), F32), jnp.ones((half,), F32)]).reshape(1, DIL_HD)
    shapes, specs = [], []
    for _, dil in DIL_PATTERNS:
        shp = (batch, dil, seq // dil, DIL_HD)
        for _ in range(2):
            shapes.append(jax.ShapeDtypeStruct(shp, F32))
            specs.append(pl.BlockSpec((None,) + shp[1:], lambda b: (b, 0, 0, 0)))
    outs = pl.pallas_call(
        functools.partial(_rope_kernel, seq=seq),
        grid=(batch,),
        in_specs=[pl.BlockSpec((None, seq, 1), lambda b: (b, 0, 0)),
                  pl.BlockSpec((1, DIL_HD), lambda b: (0, 0)),
                  pl.BlockSpec((1, DIL_HD), lambda b: (0, 0))],
        out_specs=specs,
        out_shape=shapes,
        compiler_params=_cparams(32),
        name="rope_tables",
    )(positions.reshape(batch, seq, 1), inv2, sgn)
    return [(outs[2 * g], outs[2 * g + 1]) for g in range(DIL_GROUPS)]


INPROJ_TM = 512
INPROJ_CH = 1024


def _inproj_kernel(x_ref, g_ref, w_ref, zn_ref, z1_ref, z2_ref, h_ref):
    tm = INPROJ_TM
    nlt = D_MODEL // LANES
    hf = _rms(x_ref[...], g_ref[...])
    for j in range(nlt):
        h_ref[j] = hf[:, j * LANES:(j + 1) * LANES]
    h = hf.astype(BF16)
    for c0 in range(0, ZN_COLS, INPROJ_CH):
        c1 = min(c0 + INPROJ_CH, ZN_COLS)
        zn_ref[:, c0:c1] = _dot(h, w_ref[:, c0:c1]).astype(BF16)
    for gi, z_ref in ((1, z1_ref), (2, z2_ref)):
        dil = DIL_PATTERNS[gi][1]
        n = tm // dil
        hp = jnp.concatenate(
            [jnp.concatenate([h_ref[j, pl.ds(r, n, stride=dil), :] for j in range(nlt)], axis=1)
             for r in range(dil)], axis=0).astype(BF16)
        w0 = ZN_COLS + (gi - 1) * ZG_COLS
        zg = _dot(hp, w_ref[:, w0:w0 + ZG_COLS]).astype(BF16)
        for r in range(dil):
            z_ref[r] = zg[r * n:(r + 1) * n, :]


def _inproj(x3, g, w):
    batch, seq, d = x3.shape
    tm = INPROJ_TM
    out_shapes = [jax.ShapeDtypeStruct((batch, seq, ZN_COLS), BF16)]
    out_specs = [pl.BlockSpec((None, tm, ZN_COLS), lambda b, i: (b, i, 0))]
    for _, dil in DIL_PATTERNS[1:]:
        out_shapes.append(jax.ShapeDtypeStruct((batch, dil, seq // dil, ZG_COLS), BF16))
        out_specs.append(pl.BlockSpec((None, dil, tm // dil, ZG_COLS), lambda b, i: (b, 0, i, 0)))
    return pl.pallas_call(
        _inproj_kernel,
        grid=(batch, seq // tm),
        in_specs=[pl.BlockSpec((None, tm, d), lambda b, i: (b, i, 0)),
                  _const_spec((1, d)),
                  _const_spec((d, W_COLS))],
        out_specs=out_specs,
        out_shape=out_shapes,
        scratch_shapes=[pltpu.VMEM((d // LANES, tm, LANES), F32)],
        compiler_params=_cparams(52),
        name="inproj",
    )(x3, g, w)


GLA_TS = 128
GLA_NCH = GLA_TS // GLA_CHUNK


def _band_col(h, r):
    base = (h // 2) * LANES
    if h % 2 == 0:
        return base + (LANES - r) % LANES
    return base + 2 * GLA_CHUNK - r


def _band_matrices():
    m = np.zeros((GLA_CHUNK, GLA_QK, 2 * LANES), np.float32)
    for r in range(GLA_CHUNK):
        for h in range(GLA_HEADS):
            m[r, h * GLA_DK:(h + 1) * GLA_DK, _band_col(h, r)] = 1.0
    return m


def _gla_kernel(q_ref, k_ref, v_ref, go_ref, al_ref, wup_ref, ba_ref, ng_ref, rmat_ref,
                y_ref, s_ref, *, seq):
    ts, c, nch = GLA_TS, GLA_CHUNK, GLA_NCH
    s_ref[...] = jnp.zeros(s_ref.shape, F32)

    row = lax.broadcasted_iota(jnp.int32, (ts, ts), 0)
    col = lax.broadcasted_iota(jnp.int32, (ts, ts), 1)
    same = (row >> 4) == (col >> 4)
    tri = jnp.where(same, jnp.where(col <= row, 1.0, 0.0), 0.0).astype(F32)
    blk = jnp.where(same, 1.0, 0.0).astype(F32)
    rowmod = lax.broadcasted_iota(jnp.int32, (ts, GLA_QK), 0) & (c - 1)
    keep_even = (col == 0) | (col > LANES - c)
    keep_odd = (col > c) & (col <= 2 * c)
    chunk_of_col = lax.broadcasted_iota(jnp.int32, (GLA_DK, ts), 1) >> 4
    q8_row = lax.broadcasted_iota(jnp.int32, (ts, nch * GLA_DK), 0) >> 4
    q8_col = lax.broadcasted_iota(jnp.int32, (ts, nch * GLA_DK), 1) >> 6

    def tile(ti, carry):
        r0 = pl.multiple_of(ti * ts, ts)
        q = q_ref[pl.ds(r0, ts), :].astype(F32) * (GLA_DK ** -0.5)
        k = k_ref[pl.ds(r0, ts), :].astype(F32)
        xa = _dot(al_ref[pl.ds(r0, ts), :], wup_ref[...]) + ba_ref[...]
        loga = (jnp.minimum(xa, 0.0) - jnp.log(1.0 + jnp.exp(-jnp.abs(xa)))) * (1.0 / GLA_TAU)
        lc = jnp.dot(tri, loga, precision=lax.Precision.HIGHEST, preferred_element_type=F32)
        lend = jnp.dot(blk, loga, precision=lax.Precision.HIGHEST, preferred_element_type=F32)
        q_in = q * jnp.exp(lc)
        k_in = k * jnp.exp(lend - lc)

        band = _dot((q * k).astype(BF16), rmat_ref[0])
        rel = jnp.zeros_like(loga)
        for r in range(1, c):
            rel = rel + (loga if r == 1 else pltpu.roll(loga, r - 1, 0))
            term = jnp.where(rowmod >= r, q * pltpu.roll(k, r, 0) * jnp.exp(rel), 0.0)
            band = band + _dot(term.astype(BF16), rmat_ref[r])

        kin_t = k_in.T
        lend_t = lend.T
        for h in range(GLA_HEADS):
            vh = v_ref[pl.ds(r0, ts), h * GLA_DV:(h + 1) * GLA_DV]
            bt = band[:, (h // 2) * LANES:(h // 2 + 1) * LANES]
            if h % 2 == 0:
                p = pltpu.roll(jnp.where(keep_even, bt, 0.0), 0, 1, stride=1, stride_axis=0)
            else:
                p = pltpu.roll(jnp.where(keep_odd, bt, 0.0), LANES - 2 * c, 1, stride=1, stride_axis=0)
            o = _dot(p.astype(BF16), vh)

            kt = kin_t[h * GLA_DK:(h + 1) * GLA_DK, :]
            kst = jnp.concatenate(
                [jnp.where(chunk_of_col == n, kt, 0.0) for n in range(nch)], axis=0).astype(BF16)
            u = _dot(kst, vh)
            lt = lend_t[h * GLA_DK:(h + 1) * GLA_DK, :]
            st = s_ref[h]
            states = []
            for n in range(nch):
                states.append(st)
                st = st * jnp.exp(lt[:, n * c:n * c + 1]) + u[n * GLA_DK:(n + 1) * GLA_DK, :]
            s_ref[h] = st
            sst = jnp.concatenate(states, axis=0).astype(BF16)
            qh = q_in[:, h * GLA_DK:(h + 1) * GLA_DK]
            q8 = jnp.concatenate([qh] * nch, axis=1)
            qx = jnp.where(q8_row == q8_col, q8, 0.0).astype(BF16)
            o = o + _dot(qx, sst)

            on = o * lax.rsqrt(jnp.mean(o * o, axis=-1, keepdims=True) + EPS) * ng_ref[...]
            g = go_ref[pl.ds(r0, ts), h * GLA_DV:(h + 1) * GLA_DV].astype(F32)
            y_ref[pl.ds(r0, ts), h * GLA_DV:(h + 1) * GLA_DV] = (on * g * _sigmoid(g)).astype(BF16)
        return carry

    lax.fori_loop(0, seq // ts, tile, 0)


def _gla(zn, wup, ba, ng, rmat):
    batch, seq, _ = zn.shape

    def zspec(width, blk_idx):
        return pl.BlockSpec((None, seq, width), lambda b: (b, 0, blk_idx))

    return pl.pallas_call(
        functools.partial(_gla_kernel, seq=seq),
        grid=(batch,),
        in_specs=[zspec(GLA_QK, ZB_GQ), zspec(GLA_QK, ZB_GK), zspec(GLA_V, ZB_GV),
                  zspec(GLA_V, ZB_GO), zspec(ALOW_PAD, ZB_AL),
                  _const_spec((ALOW_PAD, GLA_QK)),
                  _const_spec((1, GLA_QK)),
                  _const_spec((1, GLA_DV)),
                  _const_spec((GLA_CHUNK, GLA_QK, 2 * LANES))],
        out_specs=pl.BlockSpec((None, seq, GLA_V), lambda b: (b, 0, 0)),
        out_shape=jax.ShapeDtypeStruct((batch, seq, GLA_V), BF16),
        scratch_shapes=[pltpu.VMEM((GLA_HEADS, GLA_DK, GLA_DV), F32)],
        compiler_params=_cparams(40),
        name="gla",
    )(zn, zn, zn, zn, zn, wup, ba, ng, rmat)


def _dil_kernel(q_ref, k_ref, v_ref, cos_ref, sin_ref, o_ref, lse_ref, ks_ref, *, sub_len):
    blk = DIL_BLOCK
    nb = sub_len // blk
    scale = DIL_HD ** -0.5

    def rot(xf, cs, sn):
        return xf * cs + pltpu.roll(xf, DIL_HD // 2, 1) * sn

    for h in range(DIL_HEADS):
        sl = slice(h * DIL_HD, (h + 1) * DIL_HD)
        ks_ref[:, sl] = rot(k_ref[:, sl].astype(F32), cos_ref[...], sin_ref[...]).astype(BF16)

    qi = lax.broadcasted_iota(jnp.int32, (blk, 2 * blk), 0)
    km = lax.broadcasted_iota(jnp.int32, (blk, 2 * blk), 1)
    valid_two = (km >= qi) & (km <= qi + blk)
    valid_one = (lax.broadcasted_iota(jnp.int32, (blk, blk), 1)
                 <= lax.broadcasted_iota(jnp.int32, (blk, blk), 0))
    lane = lax.broadcasted_iota(jnp.int32, (blk, LANES), 1)
    lanes_per_head = LANES // DIL_HEADS

    def block(q0, k0, nk, valid):
        cs = cos_ref[pl.ds(q0, blk), :]
        sn = sin_ref[pl.ds(q0, blk), :]
        lse_tile = jnp.zeros((blk, LANES), F32)
        for h in range(DIL_HEADS):
            sl = slice(h * DIL_HD, (h + 1) * DIL_HD)
            qr = (rot(q_ref[pl.ds(q0, blk), sl].astype(F32), cs, sn) * scale).astype(BF16)
            kc = ks_ref[pl.ds(k0, nk), sl]
            vc = v_ref[pl.ds(k0, nk), sl]
            s = lax.dot_general(qr, kc, (((1,), (1,)), ((), ())), preferred_element_type=F32)
            s = jnp.where(valid, s, NEG_BIG)
            mx = jnp.max(s, axis=-1, keepdims=True)
            p = jnp.exp(s - mx)
            den = jnp.sum(p, axis=-1, keepdims=True)
            o = _dot(p.astype(BF16), vc) / den
            o_ref[pl.ds(q0, blk), sl] = o.astype(BF16)
            lse = mx + jnp.log(den)
            lse_tile = jnp.where(lane >= h * lanes_per_head, lse, lse_tile)
        lse_ref[pl.ds(q0, blk), :] = lse_tile

    block(0, 0, blk, valid_one)

    if nb > 1:
        def body(i, carry):
            q0 = pl.multiple_of(i * blk, blk)
            block(q0, pl.multiple_of(q0 - blk, blk), 2 * blk, valid_two)
            return carry
        lax.fori_loop(1, nb, body, 0)


def _dil_group(zq, col_blk, cos, sin):
    batch, dil, sub_len, _ = zq.shape

    def zspec(blk_idx):
        return pl.BlockSpec((None, None, sub_len, DIL_GW), lambda b, r: (b, r, 0, blk_idx))

    tspec = pl.BlockSpec((None, None, sub_len, DIL_HD), lambda b, r: (b, r, 0, 0))
    return pl.pallas_call(
        functools.partial(_dil_kernel, sub_len=sub_len),
        grid=(batch, dil),
        in_specs=[zspec(col_blk), zspec(col_blk + 1), zspec(col_blk + 2), tspec, tspec],
        out_specs=[pl.BlockSpec((None, None, sub_len, DIL_GW), lambda b, r: (b, r, 0, 0)),
                   pl.BlockSpec((None, None, sub_len, LANES), lambda b, r: (b, r, 0, 0))],
        out_shape=[jax.ShapeDtypeStruct((batch, dil, sub_len, DIL_GW), BF16),
                   jax.ShapeDtypeStruct((batch, dil, sub_len, LANES), F32)],
        scratch_shapes=[pltpu.VMEM((sub_len, DIL_GW), BF16)],
        compiler_params=_cparams(40),
        name=f"dil_attn_d{dil}",
    )(zq, zq, zq, cos, sin)


MERGE_TM = 512


def _shifted(prev_rows, cur, tm):
    full = jnp.concatenate([prev_rows, cur], axis=0)
    d1 = pltpu.roll(full, 1, 0)[SUBLANES:SUBLANES + tm]
    d2 = pltpu.roll(full, 2, 0)[SUBLANES:SUBLANES + tm]
    return d1, d2


def _merge_kernel(x_ref, scb_ref, scc_ref, scx_ref, ygla_ref, o0_ref, o1_ref, o2_ref,
                  l0_ref, l1_ref, l2_ref, gpre_ref, wg_ref, bg_ref, wbr_ref, wmix_ref,
                  gpost_ref, cw_ref, out_ref, halo_ref, oi1_ref, oi2_ref, li1_ref, li2_ref):
    tm = MERGE_TM

    @pl.when(pl.program_id(1) == 0)
    def _():
        halo_ref[...] = jnp.zeros(halo_ref.shape, F32)

    x = x_ref[...]
    h = _rms(x, gpre_ref[...]).astype(BF16)

    u = scc_ref[...].astype(F32) * scx_ref[...].astype(F32)
    u1, u2 = _shifted(halo_ref[...], u, tm)
    halo_ref[...] = u[tm - SUBLANES:, :]
    cw = cw_ref[...]
    y_sc = scb_ref[...].astype(F32) * (cw[0:1, :] * u2 + cw[1:2, :] * u1 + cw[2:3, :] * u)

    for (o_ref, l_ref, oi_ref, li_ref), (_, dil) in zip(
            ((o1_ref, l1_ref, oi1_ref, li1_ref), (o2_ref, l2_ref, oi2_ref, li2_ref)), DIL_PATTERNS[1:]):
        n = tm // dil
        for r in range(dil):
            o_r = o_ref[r].astype(F32)
            for hh in range(DIL_HEADS):
                oi_ref[hh, pl.ds(r, n, stride=dil), :] = o_r[:, hh * DIL_HD:(hh + 1) * DIL_HD]
            li_ref[pl.ds(r, n, stride=dil), :] = l_ref[r]

    lse = (l0_ref[0], li1_ref[...], li2_ref[...])
    lanes_per_head = LANES // DIL_HEADS
    parts = []
    for hh in range(DIL_HEADS):
        sl = slice(hh * DIL_HD, (hh + 1) * DIL_HD)
        la, lb, lc = (v[:, hh * lanes_per_head:hh * lanes_per_head + 1] for v in lse)
        m = jnp.maximum(jnp.maximum(la, lb), lc)
        ea, eb, ec = jnp.exp(la - m), jnp.exp(lb - m), jnp.exp(lc - m)
        inv = 1.0 / (ea + eb + ec)
        parts.append((ea * inv) * o0_ref[0, :, sl].astype(F32)
                     + (eb * inv) * oi1_ref[hh] + (ec * inv) * oi2_ref[hh])
    y_dil = jnp.concatenate(parts, axis=1)

    merged = jnp.zeros((tm, D_MODEL), F32)
    branches = (ygla_ref[...], y_sc.astype(BF16), y_dil.astype(BF16))
    for g, br in enumerate(branches):
        cs = slice(g * D_MODEL, (g + 1) * D_MODEL)
        gate = _sigmoid(_dot(h, wg_ref[:, cs]) + bg_ref[:, cs])
        merged = merged + gate * _dot(br, wbr_ref[g])
    mix = _dot(merged.astype(BF16), wmix_ref[...])
    out_ref[...] = x + _rms(mix, gpost_ref[...])


def _merge(x3, zn, ygla, o_list, l_list, gpre, wg, bg, wbr, wmix, gpost, cw):
    batch, seq, d = x3.shape
    tm = MERGE_TM

    def rows(width, blk_idx=0):
        return pl.BlockSpec((None, tm, width), lambda b, i: (b, i, blk_idx))

    def grouped(dil, width):
        return pl.BlockSpec((None, dil, tm // dil, width), lambda b, i: (b, 0, i, 0))

    dils = [dil for _, dil in DIL_PATTERNS]
    return pl.pallas_call(
        _merge_kernel,
        grid=(batch, seq // tm),
        in_specs=[rows(d),
                  rows(BRANCH_WIDTH, ZB_SCB), rows(BRANCH_WIDTH, ZB_SCC), rows(BRANCH_WIDTH, ZB_SCX),
                  rows(GLA_V)]
                 + [grouped(dil, DIL_GW) for dil in dils]
                 + [grouped(dil, LANES) for dil in dils]
                 + [_const_spec((1, d)),
                    _const_spec((d, 3 * d)),
                    _const_spec((1, 3 * d)),
                    _const_spec((3, BRANCH_WIDTH, d)),
                    _const_spec((d, d)),
                    _const_spec((1, d)),
                    _const_spec((CONV_K, BRANCH_WIDTH))],
        out_specs=rows(d),
        out_shape=jax.ShapeDtypeStruct((batch, seq, d), F32),
        scratch_shapes=[pltpu.VMEM((SUBLANES, BRANCH_WIDTH), F32),
                        pltpu.VMEM((DIL_HEADS, tm, DIL_HD), F32), pltpu.VMEM((DIL_HEADS, tm, DIL_HD), F32),
                        pltpu.VMEM((tm, LANES), F32), pltpu.VMEM((tm, LANES), F32)],
        compiler_params=_cparams(48, ("arbitrary", "arbitrary")),
        name="merge",
    )(x3, zn, zn, zn, ygla, *o_list, *l_list, gpre, wg, bg, wbr, wmix, gpost, cw)


FFN_TM = 512
FFN_CH = 1408


def _ffn_kernel(x_ref, g1_ref, wg_ref, wu_ref, cw_ref, cb_ref, wd_ref, g2_ref, out_ref, halo_ref):
    tm = FFN_TM

    @pl.when(pl.program_id(1) == 0)
    def _():
        halo_ref[...] = jnp.zeros(halo_ref.shape, F32)

    x = x_ref[...]
    h = _rms(x, g1_ref[...]).astype(BF16)
    acc = jnp.zeros((tm, D_MODEL), F32)
    for c0 in range(0, D_FF, FFN_CH):
        cs = slice(c0, c0 + FFN_CH)
        a = _dot(h, wg_ref[:, cs])
        a1, a2 = _shifted(halo_ref[:, cs], a, tm)
        halo_ref[:, cs] = a[tm - SUBLANES:, :]
        cw = cw_ref[:, cs]
        gt = cw[0:1, :] * a2 + cw[1:2, :] * a1 + cw[2:3, :] * a + cb_ref[:, cs]
        ge = 0.5 * gt * (1.0 + jnp.tanh(0.7978845608028654 * (gt + 0.044715 * (gt * gt * gt))))
        up = _dot(h, wu_ref[:, cs])
        acc = acc + _dot((ge * up).astype(BF16), wd_ref[cs, :])
    out_ref[...] = x + _rms(acc, g2_ref[...])


def _ffn(x3, g1, wg, wu, cw, cb, wd, g2):
    batch, seq, d = x3.shape
    tm = FFN_TM
    xspec = pl.BlockSpec((None, tm, d), lambda b, i: (b, i, 0))
    return pl.pallas_call(
        _ffn_kernel,
        grid=(batch, seq // tm),
        in_specs=[xspec,
                  _const_spec((1, d)),
                  _const_spec((d, D_FF)),
                  _const_spec((d, D_FF)),
                  _const_spec((CONV_K, D_FF)),
                  _const_spec((1, D_FF)),
                  _const_spec((D_FF, d)),
                  _const_spec((1, d))],
        out_specs=xspec,
        out_shape=jax.ShapeDtypeStruct((batch, seq, d), F32),
        scratch_shapes=[pltpu.VMEM((SUBLANES, D_FF), F32)],
        compiler_params=_cparams(52, ("arbitrary", "arbitrary")),
        name="ffn",
    )(x3, g1, wg, wu, cw, cb, wd, g2)


def _permute_w_in(w):
    dq, dk, dv = 3088, 4624, 6160

    def grp(gi):
        return [w[:, o + gi * DIL_GW:o + (gi + 1) * DIL_GW] for o in (dq, dk, dv)]

    pad = jnp.zeros((w.shape[0], ALOW_PAD - GLA_RANK), w.dtype)
    cols = grp(0) + [w[:, 512:1536], w[:, 1552:3088], w[:, 0:512], w[:, 1536:1552], pad] + grp(1) + grp(2)
    return jnp.concatenate(cols, axis=1).astype(BF16)


def kernel(x, positions, w_in, w_alpha_up, b_alpha, gla_norm_g, sc_conv_w, w_gate, b_gate, w_branch,
           w_mix_out, pre_mix_g, post_mix_g, pre_ffn_g, post_ffn_g, w_ff_gate, w_ff_up, ff_conv_w,
           ff_conv_b, w_ff_down):
    batch, seq, d = x.shape
    depth = w_in.shape[0]
    tables = _rope_tables(positions)
    rmat = jnp.asarray(_band_matrices(), BF16)

    for l in range(depth):
        zn, z1, z2 = _inproj(x, pre_mix_g[l].reshape(1, d), _permute_w_in(w_in[l]))
        wup = jnp.concatenate(
            [w_alpha_up[l], jnp.zeros((ALOW_PAD - GLA_RANK, GLA_QK), F32)], axis=0).astype(BF16)
        ygla = _gla(zn, wup, b_alpha[l].reshape(1, GLA_QK), gla_norm_g[l].reshape(1, GLA_DV), rmat)
        o_list, l_list = [], []
        for zq, col_blk, (cos, sin) in zip((zn.reshape(batch, 1, seq, ZN_COLS), z1, z2), (0, 0, 0), tables):
            o, lse = _dil_group(zq, col_blk, cos, sin)
            o_list.append(o)
            l_list.append(lse)
        x = _merge(x, zn, ygla, o_list, l_list, pre_mix_g[l].reshape(1, d),
                   w_gate[l].astype(BF16), b_gate[l].reshape(1, 3 * d), w_branch[l].astype(BF16),
                   w_mix_out[l].astype(BF16), post_mix_g[l].reshape(1, d), sc_conv_w[l])
        x = _ffn(x, pre_ffn_g[l].reshape(1, d), w_ff_gate[l].astype(BF16), w_ff_up[l].astype(BF16),
                 ff_conv_w[l], ff_conv_b[l].reshape(1, D_FF), w_ff_down[l].astype(BF16),
                 post_ffn_g[l].reshape(1, d))
    return x
```

```python
import functools

import jax
import jax.numpy as jnp
import numpy as np
from jax import lax
from jax.experimental import pallas as pl
from jax.experimental.pallas import tpu as pltpu

F32 = jnp.float32
BF16 = jnp.bfloat16

D_MODEL = 1024
EPS = 1e-6
BRANCH_WIDTH = 512
GLA_HEADS = 4
GLA_DV = 128
GLA_DK = 64
GLA_RANK = 16
GLA_TAU = 16.0
GLA_CHUNK = 16
GLA_QK = GLA_HEADS * GLA_DK
GLA_V = GLA_HEADS * GLA_DV
CONV_K = 3
DIL_PATTERNS = ((128, 1), (512, 4), (2048, 16))
DIL_GROUPS = len(DIL_PATTERNS)
DIL_HEADS = 4
DIL_HD = 128
DIL_GW = DIL_HEADS * DIL_HD
DIL_BLOCK = 128
ROPE_THETA = 10000.0
D_FF = 2816

LANES = 128
SUBLANES = 8
MIB = 1024 * 1024

ALOW_PAD = LANES
ZN_COLS = 2 * GLA_V + 3 * BRANCH_WIDTH + 2 * GLA_QK + ALOW_PAD
ZB_GV, ZB_GO, ZB_SCB, ZB_SCC, ZB_SCX = 0, 1, 2, 3, 4
ZB_GQ, ZB_GK = 10, 11
ZB_AL = 24
ZG_COLS = 3 * DIL_GW
W_COLS = DIL_GROUPS * ZG_COLS + ZN_COLS

NEG_BIG = -1e30


def _cparams(vmem_mib, sem=None):
    return pltpu.CompilerParams(dimension_semantics=sem, vmem_limit_bytes=int(vmem_mib * MIB))


def _const_spec(shape):
    nd = len(shape)
    return pl.BlockSpec(shape, lambda *_: (0,) * nd, pipeline_mode=pl.Buffered(1))


def _rms(xf, g):
    r = lax.rsqrt(jnp.mean(xf * xf, axis=-1, keepdims=True) + EPS)
    return xf * r * g


def _sigmoid(v):
    return 1.0 / (1.0 + jnp.exp(-v))


def _dot(a, b):
    return jnp.dot(a, b, preferred_element_type=F32)


def _rope_kernel(pos_ref, inv_ref, sgn_ref, c0_ref, s0_ref, c1_ref, s1_ref, c2_ref, s2_ref, *, seq):
    ang = pos_ref[...].astype(F32) * inv_ref[...]
    c0_ref[0] = jnp.cos(ang)
    s0_ref[0] = jnp.sin(ang) * sgn_ref[...]
    for (c_ref, s_ref), (_, dil) in zip(((c1_ref, s1_ref), (c2_ref, s2_ref)), DIL_PATTERNS[1:]):
        n = seq // dil
        for r in range(dil):
            c_ref[r] = c0_ref[0, pl.ds(r, n, stride=dil), :]
            s_ref[r] = s0_ref[0, pl.ds(r, n, stride=dil), :]


def _rope_tables(positions):
    batch, seq = positions.shape
    inv = ROPE_THETA ** (-jnp.arange(0, DIL_HD, 2, dtype=F32) / DIL_HD)
    inv2 = jnp.concatenate([inv, inv]).reshape(1, DIL_HD)
    half = DIL_HD // 2
    sgn = jnp.concatenate([-jnp.ones((half,), F32), jnp.ones((half,), F32)]).reshape(1, DIL_HD)
    shapes, specs = [], []
    for _, dil in DIL_PATTERNS:
        shp = (batch, dil, seq // dil, DIL_HD)
        for _ in range(2):
            shapes.append(jax.ShapeDtypeStruct(shp, F32))
            specs.append(pl.BlockSpec((None,) + shp[1:], lambda b: (b, 0, 0, 0)))
    outs = pl.pallas_call(
        functools.partial(_rope_kernel, seq=seq),
        grid=(batch,),
        in_specs=[pl.BlockSpec((None, seq, 1), lambda b: (b, 0, 0)),
                  pl.BlockSpec((1, DIL_HD), lambda b: (0, 0)),
                  pl.BlockSpec((1, DIL_HD), lambda b: (0, 0))],
        out_specs=specs,
        out_shape=shapes,
        compiler_params=_cparams(32),
        name="rope_tables",
    )(positions.reshape(batch, seq, 1), inv2, sgn)
    return [(outs[2 * g], outs[2 * g + 1]) for g in range(DIL_GROUPS)]


INPROJ_TM = 512
INPROJ_CH = 1024


def _inproj_kernel(x_ref, g_ref, w_ref, c0_ref, s0_ref, c1_ref, s1_ref, c2_ref, s2_ref,
                   zn_ref, z0_ref, z1_ref, z2_ref, h_ref):
    tm = INPROJ_TM
    nlt = D_MODEL // LANES
    hf = _rms(x_ref[...], g_ref[...])
    for j in range(nlt):
        h_ref[j] = hf[:, j * LANES:(j + 1) * LANES]
    h = hf.astype(BF16)

    groups = ((z0_ref, c0_ref, s0_ref), (z1_ref, c1_ref, s1_ref), (z2_ref, c2_ref, s2_ref))
    for gi, (z_ref, c_ref, s_ref) in enumerate(groups):
        dil = DIL_PATTERNS[gi][1]
        n = tm // dil
        if dil == 1:
            hp = h
        else:
            hp = jnp.concatenate(
                [jnp.concatenate([h_ref[j, pl.ds(r, n, stride=dil), :] for j in range(nlt)], axis=1)
                 for r in range(dil)], axis=0).astype(BF16)
        cs = jnp.concatenate([c_ref[r] for r in range(dil)], axis=0)
        sn = jnp.concatenate([s_ref[r] for r in range(dil)], axis=0)
        for part in range(3):
            w0 = gi * ZG_COLS + part * DIL_GW
            res = _dot(hp, w_ref[:, w0:w0 + DIL_GW])
            if part < 2:
                sc = DIL_HD ** -0.5 if part == 0 else 1.0
                heads = []
                for hh in range(DIL_HEADS):
                    xh = res[:, hh * DIL_HD:(hh + 1) * DIL_HD]
                    heads.append(xh * (cs * sc) + pltpu.roll(xh, DIL_HD // 2, 1) * (sn * sc))
                res = jnp.concatenate(heads, axis=1)
            resb = res.astype(BF16)
            for r in range(dil):
                z_ref[r, :, part * DIL_GW:(part + 1) * DIL_GW] = resb[r * n:(r + 1) * n, :]

    wn = DIL_GROUPS * ZG_COLS
    for c0 in range(0, ZN_COLS, INPROJ_CH):
        c1 = min(c0 + INPROJ_CH, ZN_COLS)
        zn_ref[:, c0:c1] = _dot(h, w_ref[:, wn + c0:wn + c1]).astype(BF16)


def _inproj(x3, g, w, tables):
    batch, seq, d = x3.shape
    tm = INPROJ_TM
    out_shapes = [jax.ShapeDtypeStruct((batch, seq, ZN_COLS), BF16)]
    out_specs = [pl.BlockSpec((None, tm, ZN_COLS), lambda b, i: (b, i, 0))]
    tab_specs, tab_args = [], []
    for (_, dil), (cos, sin) in zip(DIL_PATTERNS, tables):
        out_shapes.append(jax.ShapeDtypeStruct((batch, dil, seq // dil, ZG_COLS), BF16))
        out_specs.append(pl.BlockSpec((None, dil, tm // dil, ZG_COLS), lambda b, i: (b, 0, i, 0)))
        tab_specs += [pl.BlockSpec((None, dil, tm // dil, DIL_HD), lambda b, i: (b, 0, i, 0))] * 2
        tab_args += [cos, sin]
    return pl.pallas_call(
        _inproj_kernel,
        grid=(batch, seq // tm),
        in_specs=[pl.BlockSpec((None, tm, d), lambda b, i: (b, i, 0)),
                  _const_spec((1, d)),
                  _const_spec((d, W_COLS))] + tab_specs,
        out_specs=out_specs,
        out_shape=out_shapes,
        scratch_shapes=[pltpu.VMEM((d // LANES, tm, LANES), F32)],
        compiler_params=_cparams(52),
        name="inproj",
    )(x3, g, w, *tab_args)


GLA_TS = 128
GLA_NCH = GLA_TS // GLA_CHUNK
GLA_FAST_SPAN = 60.0


def _band_col(h, r):
    base = (h // 2) * LANES
    if h % 2 == 0:
        return base + (LANES - r) % LANES
    return base + 2 * GLA_CHUNK - r


def _band_matrices():
    m = np.zeros((GLA_CHUNK, GLA_QK, 2 * LANES), np.float32)
    for r in range(GLA_CHUNK):
        for h in range(GLA_HEADS):
            m[r, h * GLA_DK:(h + 1) * GLA_DK, _band_col(h, r)] = 1.0
    return m


def _gla_kernel(q_ref, k_ref, v_ref, go_ref, al_ref, wup_ref, ba_ref, ng_ref, rmat_ref,
                y_ref, s_ref, st_ref, la_ref, *, seq):
    ts, c, nch = GLA_TS, GLA_CHUNK, GLA_NCH
    s_ref[...] = jnp.zeros(s_ref.shape, F32)
    st_ref[...] = jnp.zeros(st_ref.shape, F32)

    def gate_tile(ti, worst):
        r0 = pl.multiple_of(ti * ts, ts)
        xa = _dot(al_ref[pl.ds(r0, ts), :], wup_ref[...]) + ba_ref[...]
        loga = (jnp.minimum(xa, 0.0) - jnp.log(1.0 + jnp.exp(-jnp.abs(xa)))) * (1.0 / GLA_TAU)
        la_ref[pl.ds(r0, ts), :] = loga
        return jnp.maximum(worst, -jnp.sum(loga, axis=0, keepdims=True))

    worst = lax.fori_loop(0, seq // ts, gate_tile, jnp.zeros((1, GLA_QK), F32))
    decay_span = jnp.max(worst)

    row = lax.broadcasted_iota(jnp.int32, (ts, ts), 0)
    col = lax.broadcasted_iota(jnp.int32, (ts, ts), 1)
    same = (row >> 4) == (col >> 4)
    tri = jnp.where(same, jnp.where(col <= row, 1.0, 0.0), 0.0).astype(F32)
    blk = jnp.where(same, 1.0, 0.0).astype(F32)
    rowmod = lax.broadcasted_iota(jnp.int32, (ts, GLA_QK), 0) & (c - 1)
    keep_even = (col == 0) | (col > LANES - c)
    keep_odd = (col > c) & (col <= 2 * c)
    chunk_of_col = lax.broadcasted_iota(jnp.int32, (GLA_DK, ts), 1) >> 4
    q8_row = lax.broadcasted_iota(jnp.int32, (ts, nch * GLA_DK), 0) >> 4
    q8_col = lax.broadcasted_iota(jnp.int32, (ts, nch * GLA_DK), 1) >> 6

    def finish(o, r0, h):
        on = o * lax.rsqrt(jnp.mean(o * o, axis=-1, keepdims=True) + EPS) * ng_ref[...]
        g = go_ref[pl.ds(r0, ts), h * GLA_DV:(h + 1) * GLA_DV].astype(F32)
        y_ref[pl.ds(r0, ts), h * GLA_DV:(h + 1) * GLA_DV] = (on * g * _sigmoid(g)).astype(BF16)

    def tile(ti, carry):
        r0 = pl.multiple_of(ti * ts, ts)
        q = q_ref[pl.ds(r0, ts), :].astype(F32) * (GLA_DK ** -0.5)
        k = k_ref[pl.ds(r0, ts), :].astype(F32)
        loga = la_ref[pl.ds(r0, ts), :]
        lc = jnp.dot(tri, loga, precision=lax.Precision.HIGHEST, preferred_element_type=F32)
        lend = jnp.dot(blk, loga, precision=lax.Precision.HIGHEST, preferred_element_type=F32)
        q_in = q * jnp.exp(lc)
        k_in = k * jnp.exp(lend - lc)

        band = _dot((q * k).astype(BF16), rmat_ref[0])
        rel = jnp.zeros_like(loga)
        for r in range(1, c):
            rel = rel + (loga if r == 1 else pltpu.roll(loga, r - 1, 0))
            term = jnp.where(rowmod >= r, q * pltpu.roll(k, r, 0) * jnp.exp(rel), 0.0)
            band = band + _dot(term.astype(BF16), rmat_ref[r])

        kin_t = k_in.T
        lend_t = lend.T
        for h in range(GLA_HEADS):
            vh = v_ref[pl.ds(r0, ts), h * GLA_DV:(h + 1) * GLA_DV]
            bt = band[:, (h // 2) * LANES:(h // 2 + 1) * LANES]
            if h % 2 == 0:
                p = pltpu.roll(jnp.where(keep_even, bt, 0.0), 0, 1, stride=1, stride_axis=0)
            else:
                p = pltpu.roll(jnp.where(keep_odd, bt, 0.0), LANES - 2 * c, 1, stride=1, stride_axis=0)
            o = _dot(p.astype(BF16), vh)

            kt = kin_t[h * GLA_DK:(h + 1) * GLA_DK, :]
            kst = jnp.concatenate(
                [jnp.where(chunk_of_col == n, kt, 0.0) for n in range(nch)], axis=0).astype(BF16)
            u = _dot(kst, vh)
            lt = lend_t[h * GLA_DK:(h + 1) * GLA_DK, :]
            st = s_ref[h]
            states = []
            for n in range(nch):
                states.append(st)
                st = st * jnp.exp(lt[:, n * c:n * c + 1]) + u[n * GLA_DK:(n + 1) * GLA_DK, :]
            s_ref[h] = st
            sst = jnp.concatenate(states, axis=0).astype(BF16)
            qh = q_in[:, h * GLA_DK:(h + 1) * GLA_DK]
            q8 = jnp.concatenate([qh] * nch, axis=1)
            qx = jnp.where(q8_row == q8_col, q8, 0.0).astype(BF16)
            finish(o + _dot(qx, sst), r0, h)
        return carry

    causal = col <= row
    tri_full = jnp.where(causal, 1.0, 0.0).astype(BF16)

    def fast_tile(ti, carry):
        r0 = pl.multiple_of(ti * ts, ts)
        loga = la_ref[pl.ds(r0, ts), :]
        hi = loga.astype(BF16)
        rest = loga - hi.astype(F32)
        mid = rest.astype(BF16)
        lo = (rest - mid.astype(F32)).astype(BF16)
        cum = _dot(tri_full, hi) + _dot(tri_full, mid) + _dot(tri_full, lo)
        lend = cum[ts - 1:ts, :]
        dec = jnp.exp(lend)
        q = q_ref[pl.ds(r0, ts), :].astype(F32) * (GLA_DK ** -0.5)
        k = k_ref[pl.ds(r0, ts), :].astype(F32)
        qt = (q * jnp.exp(cum)).astype(BF16)
        kt = k * jnp.exp(-cum)
        ks = (kt * dec).astype(BF16)
        kt = kt.astype(BF16)
        for h in range(GLA_HEADS):
            ds = slice(h * GLA_DK, (h + 1) * GLA_DK)
            vh = v_ref[pl.ds(r0, ts), h * GLA_DV:(h + 1) * GLA_DV]
            s = lax.dot_general(qt[:, ds], kt[:, ds], (((1,), (1,)), ((), ())),
                                preferred_element_type=F32)
            p = jnp.where(causal, s, 0.0).astype(BF16)
            st = st_ref[h]
            o = _dot(p, vh) + lax.dot_general(qt[:, ds], st.astype(BF16), (((1,), (1,)), ((), ())),
                                              preferred_element_type=F32)
            st_ref[h] = st * dec[:, ds] + lax.dot_general(
                vh, ks[:, ds], (((0,), (0,)), ((), ())), preferred_element_type=F32)
            finish(o, r0, h)
        return carry

    @pl.when(decay_span <= GLA_FAST_SPAN)
    def _():
        lax.fori_loop(0, seq // ts, fast_tile, 0, unroll=4)

    @pl.when(decay_span > GLA_FAST_SPAN)
    def _():
        lax.fori_loop(0, seq // ts, tile, 0)


def _gla(zn, wup, ba, ng, rmat):
    batch, seq, _ = zn.shape

    def zspec(width, blk_idx):
        return pl.BlockSpec((None, seq, width), lambda b: (b, 0, blk_idx))

    return pl.pallas_call(
        functools.partial(_gla_kernel, seq=seq),
        grid=(batch,),
        in_specs=[zspec(GLA_QK, ZB_GQ), zspec(GLA_QK, ZB_GK), zspec(GLA_V, ZB_GV),
                  zspec(GLA_V, ZB_GO), zspec(ALOW_PAD, ZB_AL),
                  _const_spec((ALOW_PAD, GLA_QK)),
                  _const_spec((1, GLA_QK)),
                  _const_spec((1, GLA_DV)),
                  _const_spec((GLA_CHUNK, GLA_QK, 2 * LANES))],
        out_specs=pl.BlockSpec((None, seq, GLA_V), lambda b: (b, 0, 0)),
        out_shape=jax.ShapeDtypeStruct((batch, seq, GLA_V), BF16),
        scratch_shapes=[pltpu.VMEM((GLA_HEADS, GLA_DK, GLA_DV), F32),
                        pltpu.VMEM((GLA_HEADS, GLA_DV, GLA_DK), F32),
                        pltpu.VMEM((seq, GLA_QK), F32)],
        compiler_params=_cparams(40),
        name="gla",
    )(zn, zn, zn, zn, zn, wup, ba, ng, rmat)


DIL_UNROLL = 4
DIL_RES_PER_STEP = (1, 4, 1)


def _dil_kernel(q_ref, k_ref, v_ref, o_ref, lse_ref, *, n_res, sub_len, chained):
    blk = DIL_BLOCK
    nb = sub_len // blk
    nk = 2 * blk
    qi = lax.broadcasted_iota(jnp.int32, (blk, nk), 0)
    km = lax.broadcasted_iota(jnp.int32, (blk, nk), 1)
    bias_window = jnp.where((km >= qi) & (km <= qi + blk), 0.0, NEG_BIG).astype(F32)
    bias_lead = jnp.where(km <= qi, 0.0, NEG_BIG).astype(F32)
    bias_trail = jnp.where((km >= blk) & (km <= qi + blk), 0.0, NEG_BIG).astype(F32)
    ones = jnp.ones((nk, DIL_HD), BF16)
    lane = lax.broadcasted_iota(jnp.int32, (blk, LANES), 1)
    lanes_per_head = LANES // DIL_HEADS

    def unit(res, q0, k0, bias):
        lse_tile = jnp.zeros((blk, LANES), F32)
        for h in range(DIL_HEADS):
            sl = slice(h * DIL_HD, (h + 1) * DIL_HD)
            q = q_ref[res, pl.ds(q0, blk), sl]
            kc = k_ref[res, pl.ds(k0, nk), sl]
            vc = jnp.concatenate([v_ref[res, pl.ds(k0, nk), sl], ones], axis=1)
            s = lax.dot_general(q, kc, (((1,), (1,)), ((), ())), preferred_element_type=F32) + bias
            mx = jnp.max(s, axis=-1, keepdims=True)
            p = jnp.exp(s - mx)
            ov = _dot(p.astype(BF16), vc)
            den = ov[:, DIL_HD:]
            o_ref[res, pl.ds(q0, blk), sl] = (ov[:, :DIL_HD] / den).astype(BF16)
            lse_tile = jnp.where(lane >= h * lanes_per_head, mx + jnp.log(den), lse_tile)
        lse_ref[res, pl.ds(q0, blk), :] = lse_tile

    assert (n_res * nb) % DIL_UNROLL == 0

    def body(g, carry):
        for u in range(DIL_UNROLL):
            uid = g * DIL_UNROLL + u
            res = uid >> (nb.bit_length() - 1)
            i = uid & (nb - 1)
            q0 = pl.multiple_of(i * blk, blk)
            if chained:
                k0 = pl.multiple_of(jnp.maximum(q0 - blk, 0), blk)
                bias = jnp.where(i == 0, bias_lead, bias_window)
            else:
                k0 = pl.multiple_of(jnp.minimum(q0, sub_len - nk), blk)
                bias = jnp.where(i == nb - 1, bias_trail, bias_lead)
            unit(res, q0, k0, bias)
        return carry

    lax.fori_loop(0, (n_res * nb) // DIL_UNROLL, body, 0)


def _dil_group(zq, n_res, chained, name):
    batch, dil, sub_len, _ = zq.shape
    assert sub_len >= 2 * DIL_BLOCK

    def zspec(blk_idx):
        return pl.BlockSpec((None, n_res, sub_len, DIL_GW), lambda b, r: (b, r, 0, blk_idx))

    return pl.pallas_call(
        functools.partial(_dil_kernel, n_res=n_res, sub_len=sub_len, chained=chained),
        grid=(batch, dil // n_res),
        in_specs=[zspec(0), zspec(1), zspec(2)],
        out_specs=[pl.BlockSpec((None, n_res, sub_len, DIL_GW), lambda b, r: (b, r, 0, 0)),
                   pl.BlockSpec((None, n_res, sub_len, LANES), lambda b, r: (b, r, 0, 0))],
        out_shape=[jax.ShapeDtypeStruct((batch, dil, sub_len, DIL_GW), BF16),
                   jax.ShapeDtypeStruct((batch, dil, sub_len, LANES), F32)],
        compiler_params=_cparams(40),
        name=name,
    )(zq, zq, zq)


MERGE_TM = 512


def _shifted(prev_rows, cur, tm):
    full = jnp.concatenate([prev_rows, cur], axis=0)
    d1 = pltpu.roll(full, 1, 0)[SUBLANES:SUBLANES + tm]
    d2 = pltpu.roll(full, 2, 0)[SUBLANES:SUBLANES + tm]
    return d1, d2


def _merge_kernel(x_ref, scb_ref, scc_ref, scx_ref, ygla_ref, o0_ref, o1_ref, o2_ref,
                  l0_ref, l1_ref, l2_ref, gpre_ref, wg_ref, bg_ref, wbr_ref, wmix_ref,
                  gpost_ref, cw_ref, out_ref, halo_ref, oi1_ref, oi2_ref, li1_ref, li2_ref):
    tm = MERGE_TM

    @pl.when(pl.program_id(1) == 0)
    def _():
        halo_ref[...] = jnp.zeros(halo_ref.shape, F32)

    x = x_ref[...]
    h = _rms(x, gpre_ref[...]).astype(BF16)

    u = scc_ref[...].astype(F32) * scx_ref[...].astype(F32)
    u1, u2 = _shifted(halo_ref[...], u, tm)
    halo_ref[...] = u[tm - SUBLANES:, :]
    cw = cw_ref[...]
    y_sc = scb_ref[...].astype(F32) * (cw[0:1, :] * u2 + cw[1:2, :] * u1 + cw[2:3, :] * u)

    for (o_ref, l_ref, oi_ref, li_ref), (_, dil) in zip(
            ((o1_ref, l1_ref, oi1_ref, li1_ref), (o2_ref, l2_ref, oi2_ref, li2_ref)), DIL_PATTERNS[1:]):
        n = tm // dil
        for r in range(dil):
            o_r = o_ref[r].astype(F32)
            for hh in range(DIL_HEADS):
                oi_ref[hh, pl.ds(r, n, stride=dil), :] = o_r[:, hh * DIL_HD:(hh + 1) * DIL_HD]
            li_ref[pl.ds(r, n, stride=dil), :] = l_ref[r]

    lse = (l0_ref[0], li1_ref[...], li2_ref[...])
    lanes_per_head = LANES // DIL_HEADS
    parts = []
    for hh in range(DIL_HEADS):
        sl = slice(hh * DIL_HD, (hh + 1) * DIL_HD)
        la, lb, lc = (v[:, hh * lanes_per_head:hh * lanes_per_head + 1] for v in lse)
        m = jnp.maximum(jnp.maximum(la, lb), lc)
        ea, eb, ec = jnp.exp(la - m), jnp.exp(lb - m), jnp.exp(lc - m)
        inv = 1.0 / (ea + eb + ec)
        parts.append((ea * inv) * o0_ref[0, :, sl].astype(F32)
                     + (eb * inv) * oi1_ref[hh] + (ec * inv) * oi2_ref[hh])
    y_dil = jnp.concatenate(parts, axis=1)

    merged = jnp.zeros((tm, D_MODEL), F32)
    branches = (ygla_ref[...], y_sc.astype(BF16), y_dil.astype(BF16))
    for g, br in enumerate(branches):
        cs = slice(g * D_MODEL, (g + 1) * D_MODEL)
        gate = _sigmoid(_dot(h, wg_ref[:, cs]) + bg_ref[:, cs])
        merged = merged + gate * _dot(br, wbr_ref[g])
    mix = _dot(merged.astype(BF16), wmix_ref[...])
    out_ref[...] = x + _rms(mix, gpost_ref[...])


def _merge(x3, zn, ygla, o_list, l_list, gpre, wg, bg, wbr, wmix, gpost, cw):
    batch, seq, d = x3.shape
    tm = MERGE_TM

    def rows(width, blk_idx=0):
        return pl.BlockSpec((None, tm, width), lambda b, i: (b, i, blk_idx))

    def grouped(dil, width):
        return pl.BlockSpec((None, dil, tm // dil, width), lambda b, i: (b, 0, i, 0))

    dils = [dil for _, dil in DIL_PATTERNS]
    return pl.pallas_call(
        _merge_kernel,
        grid=(batch, seq // tm),
        in_specs=[rows(d),
                  rows(BRANCH_WIDTH, ZB_SCB), rows(BRANCH_WIDTH, ZB_SCC), rows(BRANCH_WIDTH, ZB_SCX),
                  rows(GLA_V)]
                 + [grouped(dil, DIL_GW) for dil in dils]
                 + [grouped(dil, LANES) for dil in dils]
                 + [_const_spec((1, d)),
                    _const_spec((d, 3 * d)),
                    _const_spec((1, 3 * d)),
                    _const_spec((3, BRANCH_WIDTH, d)),
                    _const_spec((d, d)),
                    _const_spec((1, d)),
                    _const_spec((CONV_K, BRANCH_WIDTH))],
        out_specs=rows(d),
        out_shape=jax.ShapeDtypeStruct((batch, seq, d), F32),
        scratch_shapes=[pltpu.VMEM((SUBLANES, BRANCH_WIDTH), F32),
                        pltpu.VMEM((DIL_HEADS, tm, DIL_HD), F32), pltpu.VMEM((DIL_HEADS, tm, DIL_HD), F32),
                        pltpu.VMEM((tm, LANES), F32), pltpu.VMEM((tm, LANES), F32)],
        compiler_params=_cparams(48, ("arbitrary", "arbitrary")),
        name="merge",
    )(x3, zn, zn, zn, ygla, *o_list, *l_list, gpre, wg, bg, wbr, wmix, gpost, cw)


FFN_TM = 512
FFN_CH = 1408


def _ffn_kernel(x_ref, g1_ref, wg_ref, wu_ref, cw_ref, cb_ref, wd_ref, g2_ref, out_ref, halo_ref):
    tm = FFN_TM

    @pl.when(pl.program_id(1) == 0)
    def _():
        halo_ref[...] = jnp.zeros(halo_ref.shape, F32)

    x = x_ref[...]
    h = _rms(x, g1_ref[...]).astype(BF16)
    acc = jnp.zeros((tm, D_MODEL), F32)
    for c0 in range(0, D_FF, FFN_CH):
        cs = slice(c0, c0 + FFN_CH)
        a = _dot(h, wg_ref[:, cs])
        a1, a2 = _shifted(halo_ref[:, cs], a, tm)
        halo_ref[:, cs] = a[tm - SUBLANES:, :]
        cw = cw_ref[:, cs]
        gt = cw[0:1, :] * a2 + cw[1:2, :] * a1 + cw[2:3, :] * a + cb_ref[:, cs]
        ge = 0.5 * gt * (1.0 + jnp.tanh(0.7978845608028654 * (gt + 0.044715 * (gt * gt * gt))))
        up = _dot(h, wu_ref[:, cs])
        acc = acc + _dot((ge * up).astype(BF16), wd_ref[cs, :])
    out_ref[...] = x + _rms(acc, g2_ref[...])


def _ffn(x3, g1, wg, wu, cw, cb, wd, g2):
    batch, seq, d = x3.shape
    tm = FFN_TM
    xspec = pl.BlockSpec((None, tm, d), lambda b, i: (b, i, 0))
    return pl.pallas_call(
        _ffn_kernel,
        grid=(batch, seq // tm),
        in_specs=[xspec,
                  _const_spec((1, d)),
                  _const_spec((d, D_FF)),
                  _const_spec((d, D_FF)),
                  _const_spec((CONV_K, D_FF)),
                  _const_spec((1, D_FF)),
                  _const_spec((D_FF, d)),
                  _const_spec((1, d))],
        out_specs=xspec,
        out_shape=jax.ShapeDtypeStruct((batch, seq, d), F32),
        scratch_shapes=[pltpu.VMEM((SUBLANES, D_FF), F32)],
        compiler_params=_cparams(52, ("arbitrary", "arbitrary")),
        name="ffn",
    )(x3, g1, wg, wu, cw, cb, wd, g2)


def _permute_w_in(w):
    dq, dk, dv = 3088, 4624, 6160

    def grp(gi):
        return [w[:, o + gi * DIL_GW:o + (gi + 1) * DIL_GW] for o in (dq, dk, dv)]

    pad = jnp.zeros((w.shape[0], ALOW_PAD - GLA_RANK), w.dtype)
    cols = grp(0) + grp(1) + grp(2) + [w[:, 512:1536], w[:, 1552:3088], w[:, 0:512], w[:, 1536:1552], pad]
    return jnp.concatenate(cols, axis=1).astype(BF16)


def kernel(x, positions, w_in, w_alpha_up, b_alpha, gla_norm_g, sc_conv_w, w_gate, b_gate, w_branch,
           w_mix_out, pre_mix_g, post_mix_g, pre_ffn_g, post_ffn_g, w_ff_gate, w_ff_up, ff_conv_w,
           ff_conv_b, w_ff_down):
    batch, seq, d = x.shape
    depth = w_in.shape[0]
    tables = _rope_tables(positions)
    rmat = jnp.asarray(_band_matrices(), BF16)

    for l in range(depth):
        zn, z0, z1, z2 = _inproj(x, pre_mix_g[l].reshape(1, d), _permute_w_in(w_in[l]), tables)
        wup = jnp.concatenate(
            [w_alpha_up[l], jnp.zeros((ALOW_PAD - GLA_RANK, GLA_QK), F32)], axis=0).astype(BF16)
        ygla = _gla(zn, wup, b_alpha[l].reshape(1, GLA_QK), gla_norm_g[l].reshape(1, GLA_DV), rmat)
        o_list, l_list = [], []
        for zq, n_res, (_, dil) in zip((z0, z1, z2), DIL_RES_PER_STEP, DIL_PATTERNS):
            sub_len = seq // dil
            if sub_len == DIL_BLOCK:
                o, lse = _dil_group(zq.reshape(batch, 1, seq, ZG_COLS), 1, False, f"dil_attn_d{dil}")
                o, lse = o.reshape(batch, dil, sub_len, DIL_GW), lse.reshape(batch, dil, sub_len, LANES)
            else:
                o, lse = _dil_group(zq, n_res, True, f"dil_attn_d{dil}")
            o_list.append(o)
            l_list.append(lse)
        x = _merge(x, zn, ygla, o_list, l_list, pre_mix_g[l].reshape(1, d),
                   w_gate[l].astype(BF16), b_gate[l].reshape(1, 3 * d), w_branch[l].astype(BF16),
                   w_mix_out[l].astype(BF16), post_mix_g[l].reshape(1, d), sc_conv_w[l])
        x = _ffn(x, pre_ffn_g[l].reshape(1, d), w_ff_gate[l].astype(BF16), w_ff_up[l].astype(BF16),
                 ff_conv_w[l], ff_conv_b[l].reshape(1, D_FF), w_ff_down[l].astype(BF16),
                 post_ffn_g[l].reshape(1, d))
    return x
```

```python
import functools

import jax
import jax.numpy as jnp
import numpy as np
from jax import lax
from jax.experimental import pallas as pl
from jax.experimental.pallas import tpu as pltpu

F32 = jnp.float32
BF16 = jnp.bfloat16

D_MODEL = 1024
EPS = 1e-6
BRANCH_WIDTH = 512
GLA_HEADS = 4
GLA_DV = 128
GLA_DK = 64
GLA_RANK = 16
GLA_TAU = 16.0
GLA_CHUNK = 16
GLA_QK = GLA_HEADS * GLA_DK
GLA_V = GLA_HEADS * GLA_DV
CONV_K = 3
DIL_PATTERNS = ((128, 1), (512, 4), (2048, 16))
DIL_GROUPS = len(DIL_PATTERNS)
DIL_HEADS = 4
DIL_HD = 128
DIL_GW = DIL_HEADS * DIL_HD
DIL_BLOCK = 128
ROPE_THETA = 10000.0
D_FF = 2816

LANES = 128
SUBLANES = 8
MIB = 1024 * 1024

ALOW_PAD = LANES
ZN_COLS = 2 * GLA_V + 3 * BRANCH_WIDTH + 2 * GLA_QK + ALOW_PAD
ZB_GV, ZB_GO, ZB_SCB, ZB_SCC, ZB_SCX = 0, 1, 2, 3, 4
ZB_GQ, ZB_GK = 10, 11
ZB_AL = 24
ZG_COLS = 3 * DIL_GW
W_COLS = DIL_GROUPS * ZG_COLS + ZN_COLS

NEG_BIG = -1e30


def _cparams(vmem_mib, sem=None):
    return pltpu.CompilerParams(dimension_semantics=sem, vmem_limit_bytes=int(vmem_mib * MIB))


def _const_spec(shape):
    nd = len(shape)
    return pl.BlockSpec(shape, lambda *_: (0,) * nd, pipeline_mode=pl.Buffered(1))


def _rms(xf, g):
    r = lax.rsqrt(jnp.mean(xf * xf, axis=-1, keepdims=True) + EPS)
    return xf * r * g


def _sigmoid(v):
    return 1.0 / (1.0 + jnp.exp(-v))


def _dot(a, b):
    return jnp.dot(a, b, preferred_element_type=F32)


def _rope_kernel(pos_ref, inv_ref, sgn_ref, c0_ref, s0_ref, c1_ref, s1_ref, c2_ref, s2_ref, *, seq):
    half = seq // 2
    lane = lax.broadcasted_iota(jnp.int32, (half, DIL_HD), 1)
    low = lane < DIL_HD // 2
    pos = jnp.where(low, pos_ref[0:half, :], pos_ref[half:seq, :]).astype(F32)
    ang = pos * inv_ref[...]
    cs, sn = jnp.cos(ang), jnp.sin(ang)
    cs_sw, sn_sw = pltpu.roll(cs, DIL_HD // 2, 1), pltpu.roll(sn, DIL_HD // 2, 1)
    c0_ref[0, 0:half, :] = jnp.where(low, cs, cs_sw)
    c0_ref[0, half:seq, :] = jnp.where(low, cs_sw, cs)
    s0_ref[0, 0:half, :] = jnp.where(low, sn, sn_sw) * sgn_ref[...]
    s0_ref[0, half:seq, :] = jnp.where(low, sn_sw, sn) * sgn_ref[...]
    for (c_ref, s_ref), (_, dil) in zip(((c1_ref, s1_ref), (c2_ref, s2_ref)), DIL_PATTERNS[1:]):
        n = seq // dil
        for r in range(dil):
            c_ref[r] = c0_ref[0, pl.ds(r, n, stride=dil), :]
            s_ref[r] = s0_ref[0, pl.ds(r, n, stride=dil), :]


def _rope_tables(positions):
    batch, seq = positions.shape
    inv = ROPE_THETA ** (-jnp.arange(0, DIL_HD, 2, dtype=F32) / DIL_HD)
    inv2 = jnp.concatenate([inv, inv]).reshape(1, DIL_HD)
    half = DIL_HD // 2
    sgn = jnp.concatenate([-jnp.ones((half,), F32), jnp.ones((half,), F32)]).reshape(1, DIL_HD)
    shapes, specs = [], []
    for _, dil in DIL_PATTERNS:
        shp = (batch, dil, seq // dil, DIL_HD)
        for _ in range(2):
            shapes.append(jax.ShapeDtypeStruct(shp, F32))
            specs.append(pl.BlockSpec((None,) + shp[1:], lambda b: (b, 0, 0, 0)))
    outs = pl.pallas_call(
        functools.partial(_rope_kernel, seq=seq),
        grid=(batch,),
        in_specs=[pl.BlockSpec((None, seq, 1), lambda b: (b, 0, 0)),
                  pl.BlockSpec((1, DIL_HD), lambda b: (0, 0)),
                  pl.BlockSpec((1, DIL_HD), lambda b: (0, 0))],
        out_specs=specs,
        out_shape=shapes,
        compiler_params=_cparams(32),
        name="rope_tables",
    )(positions.reshape(batch, seq, 1), inv2, sgn)
    return [(outs[2 * g], outs[2 * g + 1]) for g in range(DIL_GROUPS)]


INPROJ_TM = 512
INPROJ_CH = 1024


def _inproj_kernel(x_ref, g_ref, w_ref, c0_ref, s0_ref, c1_ref, s1_ref, c2_ref, s2_ref,
                   zn_ref, z0_ref, z1_ref, z2_ref, h_ref):
    tm = INPROJ_TM
    nlt = D_MODEL // LANES
    hf = _rms(x_ref[...], g_ref[...])
    for j in range(nlt):
        h_ref[j] = hf[:, j * LANES:(j + 1) * LANES]
    h = hf.astype(BF16)

    groups = ((z0_ref, c0_ref, s0_ref), (z1_ref, c1_ref, s1_ref), (z2_ref, c2_ref, s2_ref))
    for gi, (z_ref, c_ref, s_ref) in enumerate(groups):
        dil = DIL_PATTERNS[gi][1]
        n = tm // dil
        if dil == 1:
            hp = h
        else:
            hp = jnp.concatenate(
                [jnp.concatenate([h_ref[j, pl.ds(r, n, stride=dil), :] for j in range(nlt)], axis=1)
                 for r in range(dil)], axis=0).astype(BF16)
        cs = jnp.concatenate([c_ref[r] for r in range(dil)], axis=0)
        sn = jnp.concatenate([s_ref[r] for r in range(dil)], axis=0)
        for part in range(3):
            w0 = gi * ZG_COLS + part * DIL_GW
            res = _dot(hp, w_ref[:, w0:w0 + DIL_GW])
            if part < 2:
                sc = DIL_HD ** -0.5 if part == 0 else 1.0
                heads = []
                for hh in range(DIL_HEADS):
                    xh = res[:, hh * DIL_HD:(hh + 1) * DIL_HD]
                    heads.append(xh * (cs * sc) + pltpu.roll(xh, DIL_HD // 2, 1) * (sn * sc))
                res = jnp.concatenate(heads, axis=1)
            resb = res.astype(BF16)
            for r in range(dil):
                z_ref[r, :, part * DIL_GW:(part + 1) * DIL_GW] = resb[r * n:(r + 1) * n, :]

    wn = DIL_GROUPS * ZG_COLS
    for c0 in range(0, ZN_COLS, INPROJ_CH):
        c1 = min(c0 + INPROJ_CH, ZN_COLS)
        zn_ref[:, c0:c1] = _dot(h, w_ref[:, wn + c0:wn + c1]).astype(BF16)


def _inproj(x3, g, w, tables):
    batch, seq, d = x3.shape
    tm = INPROJ_TM
    out_shapes = [jax.ShapeDtypeStruct((batch, seq, ZN_COLS), BF16)]
    out_specs = [pl.BlockSpec((None, tm, ZN_COLS), lambda b, i: (b, i, 0))]
    tab_specs, tab_args = [], []
    for (_, dil), (cos, sin) in zip(DIL_PATTERNS, tables):
        out_shapes.append(jax.ShapeDtypeStruct((batch, dil, seq // dil, ZG_COLS), BF16))
        out_specs.append(pl.BlockSpec((None, dil, tm // dil, ZG_COLS), lambda b, i: (b, 0, i, 0)))
        tab_specs += [pl.BlockSpec((None, dil, tm // dil, DIL_HD), lambda b, i: (b, 0, i, 0))] * 2
        tab_args += [cos, sin]
    return pl.pallas_call(
        _inproj_kernel,
        grid=(batch, seq // tm),
        in_specs=[pl.BlockSpec((None, tm, d), lambda b, i: (b, i, 0)),
                  _const_spec((1, d)),
                  _const_spec((d, W_COLS))] + tab_specs,
        out_specs=out_specs,
        out_shape=out_shapes,
        scratch_shapes=[pltpu.VMEM((d // LANES, tm, LANES), F32)],
        compiler_params=_cparams(52),
        name="inproj",
    )(x3, g, w, *tab_args)


GLA_TS = 128
GLA_NCH = GLA_TS // GLA_CHUNK
GLA_FAST_SPAN = 60.0


def _band_col(h, r):
    base = (h // 2) * LANES
    if h % 2 == 0:
        return base + (LANES - r) % LANES
    return base + 2 * GLA_CHUNK - r


def _band_matrices():
    m = np.zeros((GLA_CHUNK, GLA_QK, 2 * LANES), np.float32)
    for r in range(GLA_CHUNK):
        for h in range(GLA_HEADS):
            m[r, h * GLA_DK:(h + 1) * GLA_DK, _band_col(h, r)] = 1.0
    return m


def _gla_kernel(q_ref, k_ref, v_ref, go_ref, al_ref, wup_ref, ba_ref, ng_ref, rmat_ref,
                y_ref, s_ref, st_ref, la_ref, *, seq):
    ts, c, nch = GLA_TS, GLA_CHUNK, GLA_NCH
    s_ref[...] = jnp.zeros(s_ref.shape, F32)
    st_ref[...] = jnp.zeros(st_ref.shape, F32)

    def gate_tile(ti, worst):
        r0 = pl.multiple_of(ti * ts, ts)
        xa = _dot(al_ref[pl.ds(r0, ts), :], wup_ref[...]) + ba_ref[...]
        loga = (jnp.minimum(xa, 0.0) - jnp.log(1.0 + jnp.exp(-jnp.abs(xa)))) * (1.0 / GLA_TAU)
        la_ref[pl.ds(r0, ts), :] = loga
        return jnp.maximum(worst, -jnp.sum(loga, axis=0, keepdims=True))

    worst = lax.fori_loop(0, seq // ts, gate_tile, jnp.zeros((1, GLA_QK), F32))
    decay_span = jnp.max(worst)

    row = lax.broadcasted_iota(jnp.int32, (ts, ts), 0)
    col = lax.broadcasted_iota(jnp.int32, (ts, ts), 1)
    same = (row >> 4) == (col >> 4)
    tri = jnp.where(same, jnp.where(col <= row, 1.0, 0.0), 0.0).astype(F32)
    blk = jnp.where(same, 1.0, 0.0).astype(F32)
    rowmod = lax.broadcasted_iota(jnp.int32, (ts, GLA_QK), 0) & (c - 1)
    keep_even = (col == 0) | (col > LANES - c)
    keep_odd = (col > c) & (col <= 2 * c)
    chunk_of_col = lax.broadcasted_iota(jnp.int32, (GLA_DK, ts), 1) >> 4
    q8_row = lax.broadcasted_iota(jnp.int32, (ts, nch * GLA_DK), 0) >> 4
    q8_col = lax.broadcasted_iota(jnp.int32, (ts, nch * GLA_DK), 1) >> 6

    def finish(o, r0, h):
        on = o * lax.rsqrt(jnp.mean(o * o, axis=-1, keepdims=True) + EPS) * ng_ref[...]
        g = go_ref[pl.ds(r0, ts), h * GLA_DV:(h + 1) * GLA_DV].astype(F32)
        y_ref[pl.ds(r0, ts), h * GLA_DV:(h + 1) * GLA_DV] = (on * g * _sigmoid(g)).astype(BF16)

    def tile(ti, carry):
        r0 = pl.multiple_of(ti * ts, ts)
        q = q_ref[pl.ds(r0, ts), :].astype(F32) * (GLA_DK ** -0.5)
        k = k_ref[pl.ds(r0, ts), :].astype(F32)
        loga = la_ref[pl.ds(r0, ts), :]
        lc = jnp.dot(tri, loga, precision=lax.Precision.HIGHEST, preferred_element_type=F32)
        lend = jnp.dot(blk, loga, precision=lax.Precision.HIGHEST, preferred_element_type=F32)
        q_in = q * jnp.exp(lc)
        k_in = k * jnp.exp(lend - lc)

        band = _dot((q * k).astype(BF16), rmat_ref[0])
        rel = jnp.zeros_like(loga)
        for r in range(1, c):
            rel = rel + (loga if r == 1 else pltpu.roll(loga, r - 1, 0))
            term = jnp.where(rowmod >= r, q * pltpu.roll(k, r, 0) * jnp.exp(rel), 0.0)
            band = band + _dot(term.astype(BF16), rmat_ref[r])

        kin_t = k_in.T
        lend_t = lend.T
        for h in range(GLA_HEADS):
            vh = v_ref[pl.ds(r0, ts), h * GLA_DV:(h + 1) * GLA_DV]
            bt = band[:, (h // 2) * LANES:(h // 2 + 1) * LANES]
            if h % 2 == 0:
                p = pltpu.roll(jnp.where(keep_even, bt, 0.0), 0, 1, stride=1, stride_axis=0)
            else:
                p = pltpu.roll(jnp.where(keep_odd, bt, 0.0), LANES - 2 * c, 1, stride=1, stride_axis=0)
            o = _dot(p.astype(BF16), vh)

            kt = kin_t[h * GLA_DK:(h + 1) * GLA_DK, :]
            kst = jnp.concatenate(
                [jnp.where(chunk_of_col == n, kt, 0.0) for n in range(nch)], axis=0).astype(BF16)
            u = _dot(kst, vh)
            lt = lend_t[h * GLA_DK:(h + 1) * GLA_DK, :]
            st = s_ref[h]
            states = []
            for n in range(nch):
                states.append(st)
                st = st * jnp.exp(lt[:, n * c:n * c + 1]) + u[n * GLA_DK:(n + 1) * GLA_DK, :]
            s_ref[h] = st
            sst = jnp.concatenate(states, axis=0).astype(BF16)
            qh = q_in[:, h * GLA_DK:(h + 1) * GLA_DK]
            q8 = jnp.concatenate([qh] * nch, axis=1)
            qx = jnp.where(q8_row == q8_col, q8, 0.0).astype(BF16)
            finish(o + _dot(qx, sst), r0, h)
        return carry

    causal = col <= row
    tri_full = jnp.where(causal, 1.0, 0.0).astype(BF16)

    def fast_tile(ti, carry):
        r0 = pl.multiple_of(ti * ts, ts)
        loga = la_ref[pl.ds(r0, ts), :]
        hi = loga.astype(BF16)
        rest = loga - hi.astype(F32)
        mid = rest.astype(BF16)
        lo = (rest - mid.astype(F32)).astype(BF16)
        cum = _dot(tri_full, hi) + _dot(tri_full, mid) + _dot(tri_full, lo)
        lend = cum[ts - 1:ts, :]
        dec = jnp.exp(lend)
        q = q_ref[pl.ds(r0, ts), :].astype(F32) * (GLA_DK ** -0.5)
        k = k_ref[pl.ds(r0, ts), :].astype(F32)
        qt = (q * jnp.exp(cum)).astype(BF16)
        kt = k * jnp.exp(-cum)
        ks = (kt * dec).astype(BF16)
        kt = kt.astype(BF16)
        for h in range(GLA_HEADS):
            ds = slice(h * GLA_DK, (h + 1) * GLA_DK)
            vh = v_ref[pl.ds(r0, ts), h * GLA_DV:(h + 1) * GLA_DV]
            s = lax.dot_general(qt[:, ds], kt[:, ds], (((1,), (1,)), ((), ())),
                                preferred_element_type=F32)
            p = jnp.where(causal, s, 0.0).astype(BF16)
            st = st_ref[h]
            o = _dot(p, vh) + lax.dot_general(qt[:, ds], st.astype(BF16), (((1,), (1,)), ((), ())),
                                              preferred_element_type=F32)
            st_ref[h] = st * dec[:, ds] + lax.dot_general(
                vh, ks[:, ds], (((0,), (0,)), ((), ())), preferred_element_type=F32)
            finish(o, r0, h)
        return carry

    @pl.when(decay_span <= GLA_FAST_SPAN)
    def _():
        lax.fori_loop(0, seq // ts, fast_tile, 0, unroll=4)

    @pl.when(decay_span > GLA_FAST_SPAN)
    def _():
        lax.fori_loop(0, seq // ts, tile, 0)


def _gla(zn, wup, ba, ng, rmat):
    batch, seq, _ = zn.shape

    def zspec(width, blk_idx):
        return pl.BlockSpec((None, seq, width), lambda b: (b, 0, blk_idx))

    return pl.pallas_call(
        functools.partial(_gla_kernel, seq=seq),
        grid=(batch,),
        in_specs=[zspec(GLA_QK, ZB_GQ), zspec(GLA_QK, ZB_GK), zspec(GLA_V, ZB_GV),
                  zspec(GLA_V, ZB_GO), zspec(ALOW_PAD, ZB_AL),
                  _const_spec((ALOW_PAD, GLA_QK)),
                  _const_spec((1, GLA_QK)),
                  _const_spec((1, GLA_DV)),
                  _const_spec((GLA_CHUNK, GLA_QK, 2 * LANES))],
        out_specs=pl.BlockSpec((None, seq, GLA_V), lambda b: (b, 0, 0)),
        out_shape=jax.ShapeDtypeStruct((batch, seq, GLA_V), BF16),
        scratch_shapes=[pltpu.VMEM((GLA_HEADS, GLA_DK, GLA_DV), F32),
                        pltpu.VMEM((GLA_HEADS, GLA_DV, GLA_DK), F32),
                        pltpu.VMEM((seq, GLA_QK), F32)],
        compiler_params=_cparams(40),
        name="gla",
    )(zn, zn, zn, zn, zn, wup, ba, ng, rmat)


DIL_UNROLL = 16
DIL_RES_PER_STEP = (1, 4, 1)


def _dil_kernel(q_ref, k_ref, v_ref, o_ref, st_ref, *, n_res, sub_len, chained):
    blk = DIL_BLOCK
    nb = sub_len // blk
    nk = 2 * blk
    qi = lax.broadcasted_iota(jnp.int32, (blk, nk), 0)
    km = lax.broadcasted_iota(jnp.int32, (blk, nk), 1)
    bias_window = jnp.where((km >= qi) & (km <= qi + blk), 0.0, NEG_BIG).astype(F32)
    bias_lead = jnp.where(km <= qi, 0.0, NEG_BIG).astype(F32)
    bias_trail = jnp.where((km >= blk) & (km <= qi + blk), 0.0, NEG_BIG).astype(F32)
    ones = jnp.ones((nk, DIL_HD), BF16)
    lane = lax.broadcasted_iota(jnp.int32, (blk, LANES), 1)
    lanes_per_head = LANES // DIL_HEADS
    max_lane = (lane & (lanes_per_head // 2)) == 0

    def unit(res, q0, k0, bias):
        stats = jnp.zeros((blk, LANES), F32)
        for h in range(DIL_HEADS):
            sl = slice(h * DIL_HD, (h + 1) * DIL_HD)
            q = q_ref[res, pl.ds(q0, blk), sl]
            kc = k_ref[res, pl.ds(k0, nk), sl]
            vc = jnp.concatenate([v_ref[res, pl.ds(k0, nk), sl], ones], axis=1)
            s = lax.dot_general(q, kc, (((1,), (1,)), ((), ())), preferred_element_type=F32) + bias
            mx = jnp.max(s, axis=-1, keepdims=True)
            p = jnp.exp(s - mx)
            ov = _dot(p.astype(BF16), vc)
            o_ref[res, pl.ds(q0, blk), sl] = ov[:, :DIL_HD].astype(BF16)
            stats = jnp.where(lane >= h * lanes_per_head, jnp.where(max_lane, mx, ov[:, DIL_HD:]), stats)
        st_ref[res, pl.ds(q0, blk), :] = stats

    assert (n_res * nb) % DIL_UNROLL == 0

    def body(g, carry):
        for u in range(DIL_UNROLL):
            uid = g * DIL_UNROLL + u
            res = uid >> (nb.bit_length() - 1)
            i = uid & (nb - 1)
            q0 = pl.multiple_of(i * blk, blk)
            if chained:
                k0 = pl.multiple_of(jnp.maximum(q0 - blk, 0), blk)
                bias = jnp.where(i == 0, bias_lead, bias_window)
            else:
                k0 = pl.multiple_of(jnp.minimum(q0, sub_len - nk), blk)
                bias = jnp.where(i == nb - 1, bias_trail, bias_lead)
            unit(res, q0, k0, bias)
        return carry

    lax.fori_loop(0, (n_res * nb) // DIL_UNROLL, body, 0)


def _dil_group(zq, n_res, chained, name):
    batch, dil, sub_len, _ = zq.shape
    assert sub_len >= 2 * DIL_BLOCK

    def zspec(blk_idx):
        return pl.BlockSpec((None, n_res, sub_len, DIL_GW), lambda b, r: (b, r, 0, blk_idx))

    return pl.pallas_call(
        functools.partial(_dil_kernel, n_res=n_res, sub_len=sub_len, chained=chained),
        grid=(batch, dil // n_res),
        in_specs=[zspec(0), zspec(1), zspec(2)],
        out_specs=[pl.BlockSpec((None, n_res, sub_len, DIL_GW), lambda b, r: (b, r, 0, 0)),
                   pl.BlockSpec((None, n_res, sub_len, LANES), lambda b, r: (b, r, 0, 0))],
        out_shape=[jax.ShapeDtypeStruct((batch, dil, sub_len, DIL_GW), BF16),
                   jax.ShapeDtypeStruct((batch, dil, sub_len, LANES), F32)],
        compiler_params=_cparams(40),
        name=name,
    )(zq, zq, zq)


MERGE_TM = 512


def _shifted(prev_rows, cur, tm):
    full = jnp.concatenate([prev_rows, cur], axis=0)
    d1 = pltpu.roll(full, 1, 0)[SUBLANES:SUBLANES + tm]
    d2 = pltpu.roll(full, 2, 0)[SUBLANES:SUBLANES + tm]
    return d1, d2


def _merge_kernel(x_ref, scb_ref, scc_ref, scx_ref, ygla_ref, o0_ref, o1_ref, o2_ref,
                  l0_ref, l1_ref, l2_ref, gpre_ref, wg_ref, bg_ref, wbr_ref, wmix_ref,
                  gpost_ref, cw_ref, out_ref, halo_ref, oi1_ref, oi2_ref, li1_ref, li2_ref):
    tm = MERGE_TM

    @pl.when(pl.program_id(1) == 0)
    def _():
        halo_ref[...] = jnp.zeros(halo_ref.shape, F32)

    x = x_ref[...]
    h = _rms(x, gpre_ref[...]).astype(BF16)

    u = scc_ref[...].astype(F32) * scx_ref[...].astype(F32)
    u1, u2 = _shifted(halo_ref[...], u, tm)
    halo_ref[...] = u[tm - SUBLANES:, :]
    cw = cw_ref[...]
    y_sc = scb_ref[...].astype(F32) * (cw[0:1, :] * u2 + cw[1:2, :] * u1 + cw[2:3, :] * u)

    for (o_ref, l_ref, oi_ref, li_ref), (_, dil) in zip(
            ((o1_ref, l1_ref, oi1_ref, li1_ref), (o2_ref, l2_ref, oi2_ref, li2_ref)), DIL_PATTERNS[1:]):
        n = tm // dil
        for r in range(dil):
            o_r = o_ref[r].astype(F32)
            for hh in range(DIL_HEADS):
                oi_ref[hh, pl.ds(r, n, stride=dil), :] = o_r[:, hh * DIL_HD:(hh + 1) * DIL_HD]
            li_ref[pl.ds(r, n, stride=dil), :] = l_ref[r]

    stats = (l0_ref[0], li1_ref[...], li2_ref[...])
    lanes_per_head = LANES // DIL_HEADS
    parts = []
    for hh in range(DIL_HEADS):
        sl = slice(hh * DIL_HD, (hh + 1) * DIL_HD)
        lm = hh * lanes_per_head
        ld = lm + lanes_per_head // 2
        ma, mb, mc = (v[:, lm:lm + 1] for v in stats)
        da, db, dc = (v[:, ld:ld + 1] for v in stats)
        m = jnp.maximum(jnp.maximum(ma, mb), mc)
        ea, eb, ec = jnp.exp(ma - m), jnp.exp(mb - m), jnp.exp(mc - m)
        inv = 1.0 / (ea * da + eb * db + ec * dc)
        parts.append((ea * inv) * o0_ref[0, :, sl].astype(F32)
                     + (eb * inv) * oi1_ref[hh] + (ec * inv) * oi2_ref[hh])
    y_dil = jnp.concatenate(parts, axis=1)

    merged = jnp.zeros((tm, D_MODEL), F32)
    branches = (ygla_ref[...], y_sc.astype(BF16), y_dil.astype(BF16))
    for g, br in enumerate(branches):
        cs = slice(g * D_MODEL, (g + 1) * D_MODEL)
        gate = _sigmoid(_dot(h, wg_ref[:, cs]) + bg_ref[:, cs])
        merged = merged + gate * _dot(br, wbr_ref[g])
    mix = _dot(merged.astype(BF16), wmix_ref[...])
    out_ref[...] = x + _rms(mix, gpost_ref[...])


def _merge(x3, zn, ygla, o_list, l_list, gpre, wg, bg, wbr, wmix, gpost, cw):
    batch, seq, d = x3.shape
    tm = MERGE_TM

    def rows(width, blk_idx=0):
        return pl.BlockSpec((None, tm, width), lambda b, i: (b, i, blk_idx))

    def grouped(dil, width):
        return pl.BlockSpec((None, dil, tm // dil, width), lambda b, i: (b, 0, i, 0))

    dils = [dil for _, dil in DIL_PATTERNS]
    return pl.pallas_call(
        _merge_kernel,
        grid=(batch, seq // tm),
        in_specs=[rows(d),
                  rows(BRANCH_WIDTH, ZB_SCB), rows(BRANCH_WIDTH, ZB_SCC), rows(BRANCH_WIDTH, ZB_SCX),
                  rows(GLA_V)]
                 + [grouped(dil, DIL_GW) for dil in dils]
                 + [grouped(dil, LANES) for dil in dils]
                 + [_const_spec((1, d)),
                    _const_spec((d, 3 * d)),
                    _const_spec((1, 3 * d)),
                    _const_spec((3, BRANCH_WIDTH, d)),
                    _const_spec((d, d)),
                    _const_spec((1, d)),
                    _const_spec((CONV_K, BRANCH_WIDTH))],
        out_specs=rows(d),
        out_shape=jax.ShapeDtypeStruct((batch, seq, d), F32),
        scratch_shapes=[pltpu.VMEM((SUBLANES, BRANCH_WIDTH), F32),
                        pltpu.VMEM((DIL_HEADS, tm, DIL_HD), F32), pltpu.VMEM((DIL_HEADS, tm, DIL_HD), F32),
                        pltpu.VMEM((tm, LANES), F32), pltpu.VMEM((tm, LANES), F32)],
        compiler_params=_cparams(48, ("arbitrary", "arbitrary")),
        name="merge",
    )(x3, zn, zn, zn, ygla, *o_list, *l_list, gpre, wg, bg, wbr, wmix, gpost, cw)


FFN_TM = 512
FFN_CHUNKS = ((0, 1536), (1536, D_FF))
FFN_ROW_GROUPS = 1


def _ffn_kernel(x_ref, g1_ref, wg_ref, wu_ref, cw_ref, cb_ref, wd_ref, g2_ref, out_ref, halo_ref):
    tm = FFN_TM

    @pl.when(pl.program_id(1) == 0)
    def _():
        halo_ref[...] = jnp.zeros(halo_ref.shape, F32)

    rows = tm // FFN_ROW_GROUPS
    prev = [halo_ref[:, c0:c1] for c0, c1 in FFN_CHUNKS]
    for s in range(FFN_ROW_GROUPS):
        rs = slice(s * rows, (s + 1) * rows)
        x = x_ref[rs, :]
        h = _rms(x, g1_ref[...]).astype(BF16)
        acc = jnp.zeros((rows, D_MODEL), F32)
        for ci, (c0, c1) in enumerate(FFN_CHUNKS):
            a = _dot(h, wg_ref[:, c0:c1])
            a1, a2 = _shifted(prev[ci], a, rows)
            prev[ci] = a[rows - SUBLANES:, :]
            cw = cw_ref[:, c0:c1]
            gt = cw[0:1, :] * a2 + cw[1:2, :] * a1 + cw[2:3, :] * a + cb_ref[:, c0:c1]
            ge = 0.5 * gt * (1.0 + jnp.tanh(0.7978845608028654 * (gt + 0.044715 * (gt * gt * gt))))
            up = _dot(h, wu_ref[:, c0:c1])
            acc = acc + _dot((ge * up).astype(BF16), wd_ref[c0:c1, :])
        out_ref[rs, :] = x + _rms(acc, g2_ref[...])
    for ci, (c0, c1) in enumerate(FFN_CHUNKS):
        halo_ref[:, c0:c1] = prev[ci]


def _ffn(x3, g1, wg, wu, cw, cb, wd, g2):
    batch, seq, d = x3.shape
    tm = FFN_TM
    xspec = pl.BlockSpec((None, tm, d), lambda b, i: (b, i, 0))
    return pl.pallas_call(
        _ffn_kernel,
        grid=(batch, seq // tm),
        in_specs=[xspec,
                  _const_spec((1, d)),
                  _const_spec((d, D_FF)),
                  _const_spec((d, D_FF)),
                  _const_spec((CONV_K, D_FF)),
                  _const_spec((1, D_FF)),
                  _const_spec((D_FF, d)),
                  _const_spec((1, d))],
        out_specs=xspec,
        out_shape=jax.ShapeDtypeStruct((batch, seq, d), F32),
        scratch_shapes=[pltpu.VMEM((SUBLANES, D_FF), F32)],
        compiler_params=_cparams(52, ("arbitrary", "arbitrary")),
        name="ffn",
    )(x3, g1, wg, wu, cw, cb, wd, g2)


def _permute_w_in(w):
    dq, dk, dv = 3088, 4624, 6160
    w = w.astype(BF16)

    def grp(gi):
        return [w[:, o + gi * DIL_GW:o + (gi + 1) * DIL_GW] for o in (dq, dk, dv)]

    pad = jnp.zeros((w.shape[0], ALOW_PAD - GLA_RANK), w.dtype)
    cols = grp(0) + grp(1) + grp(2) + [w[:, 512:1536], w[:, 1552:3088], w[:, 0:512], w[:, 1536:1552], pad]
    return jnp.concatenate(cols, axis=1)


def kernel(x, positions, w_in, w_alpha_up, b_alpha, gla_norm_g, sc_conv_w, w_gate, b_gate, w_branch,
           w_mix_out, pre_mix_g, post_mix_g, pre_ffn_g, post_ffn_g, w_ff_gate, w_ff_up, ff_conv_w,
           ff_conv_b, w_ff_down):
    batch, seq, d = x.shape
    depth = w_in.shape[0]
    tables = _rope_tables(positions)
    rmat = jnp.asarray(_band_matrices(), BF16)

    for l in range(depth):
        zn, z0, z1, z2 = _inproj(x, pre_mix_g[l].reshape(1, d), _permute_w_in(w_in[l]), tables)
        wup = jnp.concatenate(
            [w_alpha_up[l], jnp.zeros((ALOW_PAD - GLA_RANK, GLA_QK), F32)], axis=0).astype(BF16)
        ygla = _gla(zn, wup, b_alpha[l].reshape(1, GLA_QK), gla_norm_g[l].reshape(1, GLA_DV), rmat)
        o_list, l_list = [], []
        for zq, n_res, (_, dil) in zip((z0, z1, z2), DIL_RES_PER_STEP, DIL_PATTERNS):
            sub_len = seq // dil
            if sub_len == DIL_BLOCK:
                o, lse = _dil_group(zq.reshape(batch, 1, seq, ZG_COLS), 1, False, f"dil_attn_d{dil}")
                o, lse = o.reshape(batch, dil, sub_len, DIL_GW), lse.reshape(batch, dil, sub_len, LANES)
            else:
                o, lse = _dil_group(zq, n_res, True, f"dil_attn_d{dil}")
            o_list.append(o)
            l_list.append(lse)
        x = _merge(x, zn, ygla, o_list, l_list, pre_mix_g[l].reshape(1, d),
                   w_gate[l].astype(BF16), b_gate[l].reshape(1, 3 * d), w_branch[l].astype(BF16),
                   w_mix_out[l].astype(BF16), post_mix_g[l].reshape(1, d), sc_conv_w[l])
        x = _ffn(x, pre_ffn_g[l].reshape(1, d), w_ff_gate[l].astype(BF16), w_ff_up[l].astype(BF16),
                 ff_conv_w[l], ff_conv_b[l].reshape(1, D_FF), w_ff_down[l].astype(BF16),
                 post_ffn_g[l].reshape(1, d))
    return x
```

```python
import functools

import jax
import jax.numpy as jnp
import numpy as np
from jax import lax
from jax.experimental import pallas as pl
from jax.experimental.pallas import tpu as pltpu

F32 = jnp.float32
BF16 = jnp.bfloat16

D_MODEL = 1024
EPS = 1e-6
BRANCH_WIDTH = 512
GLA_HEADS = 4
GLA_DV = 128
GLA_DK = 64
GLA_RANK = 16
GLA_TAU = 16.0
GLA_CHUNK = 16
GLA_QK = GLA_HEADS * GLA_DK
GLA_V = GLA_HEADS * GLA_DV
CONV_K = 3
DIL_PATTERNS = ((128, 1), (512, 4), (2048, 16))
DIL_GROUPS = len(DIL_PATTERNS)
DIL_HEADS = 4
DIL_HD = 128
DIL_GW = DIL_HEADS * DIL_HD
DIL_BLOCK = 128
ROPE_THETA = 10000.0
D_FF = 2816

LANES = 128
SUBLANES = 8
MIB = 1024 * 1024

ALOW_PAD = LANES
ZN_COLS = 2 * GLA_V + 3 * BRANCH_WIDTH + 2 * GLA_QK + ALOW_PAD
ZB_GV, ZB_GO, ZB_SCB, ZB_SCC, ZB_SCX = 0, 1, 2, 3, 4
ZB_GQ, ZB_GK = 10, 11
ZB_AL = 24
ZG_COLS = 3 * DIL_GW
W_COLS = DIL_GROUPS * ZG_COLS + ZN_COLS

NEG_BIG = -1e30


def _cparams(vmem_mib, sem=None):
    return pltpu.CompilerParams(dimension_semantics=sem, vmem_limit_bytes=int(vmem_mib * MIB))


def _const_spec(shape):
    nd = len(shape)
    return pl.BlockSpec(shape, lambda *_: (0,) * nd, pipeline_mode=pl.Buffered(1))


def _rms(xf, g):
    r = lax.rsqrt(jnp.mean(xf * xf, axis=-1, keepdims=True) + EPS)
    return xf * r * g


def _sigmoid(v):
    return 1.0 / (1.0 + jnp.exp(-v))


def _dot(a, b):
    return jnp.dot(a, b, preferred_element_type=F32)


def _dot_nt(a, b):
    return lax.dot_general(a, b, (((1,), (1,)), ((), ())), preferred_element_type=F32)


def _dot_tn(a, b):
    return lax.dot_general(a, b, (((0,), (0,)), ((), ())), preferred_element_type=F32)


def _rope_kernel(pos_ref, inv_ref, sgn_ref, c0_ref, s0_ref, c1_ref, s1_ref, c2_ref, s2_ref, *, seq):
    half = seq // 2
    lane = lax.broadcasted_iota(jnp.int32, (half, DIL_HD), 1)
    low = lane < DIL_HD // 2
    pos = jnp.where(low, pos_ref[0:half, :], pos_ref[half:seq, :]).astype(F32)
    ang = pos * inv_ref[...]
    cs, sn = jnp.cos(ang), jnp.sin(ang)
    cs_sw, sn_sw = pltpu.roll(cs, DIL_HD // 2, 1), pltpu.roll(sn, DIL_HD // 2, 1)
    c0_ref[0, 0:half, :] = jnp.where(low, cs, cs_sw)
    c0_ref[0, half:seq, :] = jnp.where(low, cs_sw, cs)
    s0_ref[0, 0:half, :] = jnp.where(low, sn, sn_sw) * sgn_ref[...]
    s0_ref[0, half:seq, :] = jnp.where(low, sn_sw, sn) * sgn_ref[...]
    for (c_ref, s_ref), (_, dil) in zip(((c1_ref, s1_ref), (c2_ref, s2_ref)), DIL_PATTERNS[1:]):
        n = seq // dil
        for r in range(dil):
            c_ref[r] = c0_ref[0, pl.ds(r, n, stride=dil), :]
            s_ref[r] = s0_ref[0, pl.ds(r, n, stride=dil), :]


def _rope_tables(positions):
    batch, seq = positions.shape
    inv = ROPE_THETA ** (-jnp.arange(0, DIL_HD, 2, dtype=F32) / DIL_HD)
    inv2 = jnp.concatenate([inv, inv]).reshape(1, DIL_HD)
    half = DIL_HD // 2
    sgn = jnp.concatenate([-jnp.ones((half,), F32), jnp.ones((half,), F32)]).reshape(1, DIL_HD)
    shapes, specs = [], []
    for _, dil in DIL_PATTERNS:
        shp = (batch, dil, seq // dil, DIL_HD)
        for _ in range(2):
            shapes.append(jax.ShapeDtypeStruct(shp, F32))
            specs.append(pl.BlockSpec((None,) + shp[1:], lambda b: (b, 0, 0, 0)))
    outs = pl.pallas_call(
        functools.partial(_rope_kernel, seq=seq),
        grid=(batch,),
        in_specs=[pl.BlockSpec((None, seq, 1), lambda b: (b, 0, 0)),
                  pl.BlockSpec((1, DIL_HD), lambda b: (0, 0)),
                  pl.BlockSpec((1, DIL_HD), lambda b: (0, 0))],
        out_specs=specs,
        out_shape=shapes,
        compiler_params=_cparams(32),
        name="rope_tables",
    )(positions.reshape(batch, seq, 1), inv2, sgn)
    return [(outs[2 * g], outs[2 * g + 1]) for g in range(DIL_GROUPS)]


_W_IN_SEGMENTS = tuple((o + gi * DIL_GW, DIL_GW) for gi in range(DIL_GROUPS) for o in (3088, 4624, 6160)) + (
    (512, 2 * GLA_V), (1552, 3 * BRANCH_WIDTH), (0, 2 * GLA_QK), (1536, GLA_RANK))
W_PREP_ROWS = 128


def _w_in_kernel(w_ref, o_ref):
    dst = 0
    for src, width in _W_IN_SEGMENTS:
        o_ref[:, dst:dst + width] = w_ref[:, src:src + width].astype(BF16)
        dst += width
    o_ref[:, dst:W_COLS] = jnp.zeros((o_ref.shape[0], W_COLS - dst), BF16)


def _permute_w_in(w_in):
    depth, d, cols = w_in.shape
    return pl.pallas_call(
        _w_in_kernel,
        grid=(depth, d // W_PREP_ROWS),
        in_specs=[pl.BlockSpec((None, W_PREP_ROWS, cols), lambda l, i: (l, i, 0))],
        out_specs=pl.BlockSpec((None, W_PREP_ROWS, W_COLS), lambda l, i: (l, i, 0)),
        out_shape=jax.ShapeDtypeStruct((depth, d, W_COLS), BF16),
        compiler_params=_cparams(32),
        name="w_in_permute",
    )(w_in)


INPROJ_TM = 512
INPROJ_CH = 1024


def _inproj_kernel(x_ref, g_ref, w_ref, c0_ref, s0_ref, c1_ref, s1_ref, c2_ref, s2_ref,
                   zn_ref, z0_ref, z1_ref, z2_ref, h_ref):
    tm = INPROJ_TM
    nlt = D_MODEL // LANES
    hf = _rms(x_ref[...], g_ref[...])
    for j in range(nlt):
        h_ref[j] = hf[:, j * LANES:(j + 1) * LANES]
    h = hf.astype(BF16)

    groups = ((z0_ref, c0_ref, s0_ref), (z1_ref, c1_ref, s1_ref), (z2_ref, c2_ref, s2_ref))
    for gi, (z_ref, c_ref, s_ref) in enumerate(groups):
        dil = DIL_PATTERNS[gi][1]
        n = tm // dil
        if dil == 1:
            hp = h
        else:
            hp = jnp.concatenate(
                [jnp.concatenate([h_ref[j, pl.ds(r, n, stride=dil), :] for j in range(nlt)], axis=1)
                 for r in range(dil)], axis=0).astype(BF16)
        cs = jnp.concatenate([c_ref[r] for r in range(dil)], axis=0)
        sn = jnp.concatenate([s_ref[r] for r in range(dil)], axis=0)
        for part in range(3):
            w0 = gi * ZG_COLS + part * DIL_GW
            res = _dot(hp, w_ref[:, w0:w0 + DIL_GW])
            if part < 2:
                sc = DIL_HD ** -0.5 if part == 0 else 1.0
                heads = []
                for hh in range(DIL_HEADS):
                    xh = res[:, hh * DIL_HD:(hh + 1) * DIL_HD]
                    heads.append(xh * (cs * sc) + pltpu.roll(xh, DIL_HD // 2, 1) * (sn * sc))
                res = jnp.concatenate(heads, axis=1)
            resb = res.astype(BF16)
            for r in range(dil):
                z_ref[r, :, part * DIL_GW:(part + 1) * DIL_GW] = resb[r * n:(r + 1) * n, :]

    wn = DIL_GROUPS * ZG_COLS
    for c0 in range(0, ZN_COLS, INPROJ_CH):
        c1 = min(c0 + INPROJ_CH, ZN_COLS)
        zn_ref[:, c0:c1] = _dot(h, w_ref[:, wn + c0:wn + c1]).astype(BF16)


def _inproj(x3, g, w_all, layer, tables):
    batch, seq, d = x3.shape
    tm = INPROJ_TM
    out_shapes = [jax.ShapeDtypeStruct((batch, seq, ZN_COLS), BF16)]
    out_specs = [pl.BlockSpec((None, tm, ZN_COLS), lambda b, i: (b, i, 0))]
    tab_specs, tab_args = [], []
    for (_, dil), (cos, sin) in zip(DIL_PATTERNS, tables):
        out_shapes.append(jax.ShapeDtypeStruct((batch, dil, seq // dil, ZG_COLS), BF16))
        out_specs.append(pl.BlockSpec((None, dil, tm // dil, ZG_COLS), lambda b, i: (b, 0, i, 0)))
        tab_specs += [pl.BlockSpec((None, dil, tm // dil, DIL_HD), lambda b, i: (b, 0, i, 0))] * 2
        tab_args += [cos, sin]
    return pl.pallas_call(
        _inproj_kernel,
        grid=(batch, seq // tm),
        in_specs=[pl.BlockSpec((None, tm, d), lambda b, i: (b, i, 0)),
                  _const_spec((1, d)),
                  pl.BlockSpec((None, d, W_COLS), lambda b, i: (layer, 0, 0),
                               pipeline_mode=pl.Buffered(1))] + tab_specs,
        out_specs=out_specs,
        out_shape=out_shapes,
        scratch_shapes=[pltpu.VMEM((d // LANES, tm, LANES), F32)],
        compiler_params=_cparams(52),
        name="inproj",
    )(x3, g, w_all, *tab_args)


GLA_TS = 128
GLA_NCH = GLA_TS // GLA_CHUNK
GLA_FAST_SPAN = 60.0
GLA_FAST_TS = 256
GLA_FAST_UNROLL = 2
GLA_SEQ_PER_STEP = 2


def _band_col(h, r):
    base = (h // 2) * LANES
    if h % 2 == 0:
        return base + (LANES - r) % LANES
    return base + 2 * GLA_CHUNK - r


def _band_matrices():
    m = np.zeros((GLA_CHUNK, GLA_QK, 2 * LANES), np.float32)
    for r in range(GLA_CHUNK):
        for h in range(GLA_HEADS):
            m[r, h * GLA_DK:(h + 1) * GLA_DK, _band_col(h, r)] = 1.0
    return m


def _gla_kernel(q_ref, k_ref, v_ref, go_ref, al_ref, wup_ref, ba_ref, ng_ref, rmat_ref,
                y_ref, s_ref, st_ref, la_ref, *, seq):
    ts, c, nch, nseq = GLA_TS, GLA_CHUNK, GLA_NCH, GLA_SEQ_PER_STEP
    s_ref[...] = jnp.zeros(s_ref.shape, F32)
    st_ref[...] = jnp.zeros(st_ref.shape, F32)

    def gate_tile(ti, worst):
        r0 = pl.multiple_of(ti * ts, ts)
        for b in range(nseq):
            xa = _dot(al_ref[b, pl.ds(r0, ts), :], wup_ref[...]) + ba_ref[...]
            loga = (jnp.minimum(xa, 0.0) - jnp.log(1.0 + jnp.exp(-jnp.abs(xa)))) * (1.0 / GLA_TAU)
            la_ref[b, pl.ds(r0, ts), :] = loga
            worst = jnp.maximum(worst, -jnp.sum(loga, axis=0, keepdims=True))
        return worst

    worst = lax.fori_loop(0, seq // ts, gate_tile, jnp.zeros((1, GLA_QK), F32))
    decay_span = jnp.max(worst)

    row = lax.broadcasted_iota(jnp.int32, (ts, ts), 0)
    col = lax.broadcasted_iota(jnp.int32, (ts, ts), 1)
    same = (row >> 4) == (col >> 4)
    tri = jnp.where(same, jnp.where(col <= row, 1.0, 0.0), 0.0).astype(F32)
    blk = jnp.where(same, 1.0, 0.0).astype(F32)
    rowmod = lax.broadcasted_iota(jnp.int32, (ts, GLA_QK), 0) & (c - 1)
    keep_even = (col == 0) | (col > LANES - c)
    keep_odd = (col > c) & (col <= 2 * c)
    chunk_of_col = lax.broadcasted_iota(jnp.int32, (GLA_DK, ts), 1) >> 4
    q8_row = lax.broadcasted_iota(jnp.int32, (ts, nch * GLA_DK), 0) >> 4
    q8_col = lax.broadcasted_iota(jnp.int32, (ts, nch * GLA_DK), 1) >> 6

    def finish(o, b, r0, h, rows):
        on = o * lax.rsqrt(jnp.mean(o * o, axis=-1, keepdims=True) + EPS) * ng_ref[...]
        g = go_ref[b, pl.ds(r0, rows), h * GLA_DV:(h + 1) * GLA_DV].astype(F32)
        y_ref[b, pl.ds(r0, rows), h * GLA_DV:(h + 1) * GLA_DV] = (on * g * _sigmoid(g)).astype(BF16)

    def safe_tile(b, r0):
        q = q_ref[b, pl.ds(r0, ts), :].astype(F32) * (GLA_DK ** -0.5)
        k = k_ref[b, pl.ds(r0, ts), :].astype(F32)
        loga = la_ref[b, pl.ds(r0, ts), :]
        lc = jnp.dot(tri, loga, precision=lax.Precision.HIGHEST, preferred_element_type=F32)
        lend = jnp.dot(blk, loga, precision=lax.Precision.HIGHEST, preferred_element_type=F32)
        q_in = q * jnp.exp(lc)
        k_in = k * jnp.exp(lend - lc)

        band = _dot((q * k).astype(BF16), rmat_ref[0])
        rel = jnp.zeros_like(loga)
        for r in range(1, c):
            rel = rel + (loga if r == 1 else pltpu.roll(loga, r - 1, 0))
            term = jnp.where(rowmod >= r, q * pltpu.roll(k, r, 0) * jnp.exp(rel), 0.0)
            band = band + _dot(term.astype(BF16), rmat_ref[r])

        kin_t = k_in.T
        lend_t = lend.T
        for h in range(GLA_HEADS):
            vh = v_ref[b, pl.ds(r0, ts), h * GLA_DV:(h + 1) * GLA_DV]
            bt = band[:, (h // 2) * LANES:(h // 2 + 1) * LANES]
            if h % 2 == 0:
                p = pltpu.roll(jnp.where(keep_even, bt, 0.0), 0, 1, stride=1, stride_axis=0)
            else:
                p = pltpu.roll(jnp.where(keep_odd, bt, 0.0), LANES - 2 * c, 1, stride=1, stride_axis=0)
            o = _dot(p.astype(BF16), vh)

            kt = kin_t[h * GLA_DK:(h + 1) * GLA_DK, :]
            kst = jnp.concatenate(
                [jnp.where(chunk_of_col == n, kt, 0.0) for n in range(nch)], axis=0).astype(BF16)
            u = _dot(kst, vh)
            lt = lend_t[h * GLA_DK:(h + 1) * GLA_DK, :]
            st = s_ref[b * GLA_HEADS + h]
            states = []
            for n in range(nch):
                states.append(st)
                st = st * jnp.exp(lt[:, n * c:n * c + 1]) + u[n * GLA_DK:(n + 1) * GLA_DK, :]
            s_ref[b * GLA_HEADS + h] = st
            sst = jnp.concatenate(states, axis=0).astype(BF16)
            qh = q_in[:, h * GLA_DK:(h + 1) * GLA_DK]
            q8 = jnp.concatenate([qh] * nch, axis=1)
            qx = jnp.where(q8_row == q8_col, q8, 0.0).astype(BF16)
            finish(o + _dot(qx, sst), b, r0, h, ts)

    tf = GLA_FAST_TS
    causal = (lax.broadcasted_iota(jnp.int32, (tf, tf), 1) <= lax.broadcasted_iota(jnp.int32, (tf, tf), 0))
    tri_full = jnp.where(causal, 1.0, 0.0).astype(BF16)

    def fast_tile(b, r0):
        loga = la_ref[b, pl.ds(r0, tf), :]
        hi = loga.astype(BF16)
        rest = loga - hi.astype(F32)
        mid = rest.astype(BF16)
        lo = (rest - mid.astype(F32)).astype(BF16)
        cum = _dot(tri_full, hi) + _dot(tri_full, mid) + _dot(tri_full, lo)
        lend = cum[tf - 1:tf, :]
        dec = jnp.exp(lend)
        q = q_ref[b, pl.ds(r0, tf), :].astype(F32) * (GLA_DK ** -0.5)
        k = k_ref[b, pl.ds(r0, tf), :].astype(F32)
        qt = (q * jnp.exp(cum)).astype(BF16)
        kt = k * jnp.exp(-cum)
        ks = (kt * dec).astype(BF16)
        kt = kt.astype(BF16)
        for h in range(GLA_HEADS):
            ds = slice(h * GLA_DK, (h + 1) * GLA_DK)
            vh = v_ref[b, pl.ds(r0, tf), h * GLA_DV:(h + 1) * GLA_DV]
            p = jnp.where(causal, _dot_nt(qt[:, ds], kt[:, ds]), 0.0).astype(BF16)
            st = st_ref[b * GLA_HEADS + h]
            o = _dot(p, vh) + _dot_nt(qt[:, ds], st.astype(BF16))
            st_ref[b * GLA_HEADS + h] = st * dec[:, ds] + _dot_tn(vh, ks[:, ds])
            finish(o, b, r0, h, tf)

    def run(tile_fn, rows, unroll):
        def body(ti, carry):
            r0 = pl.multiple_of(ti * rows, rows)
            for b in range(nseq):
                tile_fn(b, r0)
            return carry
        lax.fori_loop(0, seq // rows, body, 0, unroll=unroll)

    fast_ok = decay_span * (tf // ts) <= GLA_FAST_SPAN

    @pl.when(fast_ok)
    def _():
        run(fast_tile, tf, GLA_FAST_UNROLL)

    @pl.when(jnp.logical_not(fast_ok))
    def _():
        run(safe_tile, ts, 1)


def _gla(zn, wup, ba, ng, rmat):
    batch, seq, _ = zn.shape
    nseq = GLA_SEQ_PER_STEP
    assert batch % nseq == 0

    def zspec(width, blk_idx):
        return pl.BlockSpec((nseq, seq, width), lambda b: (b, 0, blk_idx))

    return pl.pallas_call(
        functools.partial(_gla_kernel, seq=seq),
        grid=(batch // nseq,),
        in_specs=[zspec(GLA_QK, ZB_GQ), zspec(GLA_QK, ZB_GK), zspec(GLA_V, ZB_GV),
                  zspec(GLA_V, ZB_GO), zspec(ALOW_PAD, ZB_AL),
                  _const_spec((ALOW_PAD, GLA_QK)),
                  _const_spec((1, GLA_QK)),
                  _const_spec((1, GLA_DV)),
                  _const_spec((GLA_CHUNK, GLA_QK, 2 * LANES))],
        out_specs=pl.BlockSpec((nseq, seq, GLA_V), lambda b: (b, 0, 0)),
        out_shape=jax.ShapeDtypeStruct((batch, seq, GLA_V), BF16),
        scratch_shapes=[pltpu.VMEM((nseq * GLA_HEADS, GLA_DK, GLA_DV), F32),
                        pltpu.VMEM((nseq * GLA_HEADS, GLA_DV, GLA_DK), F32),
                        pltpu.VMEM((nseq, seq, GLA_QK), F32)],
        compiler_params=_cparams(48),
        name="gla",
    )(zn, zn, zn, zn, zn, wup, ba, ng, rmat)


DIL_UNROLL = 16
DIL_RES_PER_STEP = (1, 4, 1)


def _dil_kernel(q_ref, k_ref, v_ref, o_ref, st_ref, *, n_res, sub_len, chained):
    blk = DIL_BLOCK
    nb = sub_len // blk
    nk = 2 * blk
    qi = lax.broadcasted_iota(jnp.int32, (blk, nk), 0)
    km = lax.broadcasted_iota(jnp.int32, (blk, nk), 1)
    bias_window = jnp.where((km >= qi) & (km <= qi + blk), 0.0, NEG_BIG).astype(F32)
    bias_lead = jnp.where(km <= qi, 0.0, NEG_BIG).astype(F32)
    bias_trail = jnp.where((km >= blk) & (km <= qi + blk), 0.0, NEG_BIG).astype(F32)
    ones = jnp.ones((nk, DIL_HD), BF16)
    lane = lax.broadcasted_iota(jnp.int32, (blk, LANES), 1)
    lanes_per_head = LANES // DIL_HEADS
    max_lane = (lane & (lanes_per_head // 2)) == 0

    def unit(res, q0, k0, bias):
        stats = jnp.zeros((blk, LANES), F32)
        for h in range(DIL_HEADS):
            sl = slice(h * DIL_HD, (h + 1) * DIL_HD)
            q = q_ref[res, pl.ds(q0, blk), sl]
            kc = k_ref[res, pl.ds(k0, nk), sl]
            vc = jnp.concatenate([v_ref[res, pl.ds(k0, nk), sl], ones], axis=1)
            s = _dot_nt(q, kc) + bias
            mx = jnp.max(s, axis=-1, keepdims=True)
            p = jnp.exp(s - mx)
            ov = _dot(p.astype(BF16), vc)
            o_ref[res, pl.ds(q0, blk), sl] = ov[:, :DIL_HD].astype(BF16)
            stats = jnp.where(lane >= h * lanes_per_head, jnp.where(max_lane, mx, ov[:, DIL_HD:]), stats)
        st_ref[res, pl.ds(q0, blk), :] = stats

    assert (n_res * nb) % DIL_UNROLL == 0

    def body(g, carry):
        for u in range(DIL_UNROLL):
            uid = g * DIL_UNROLL + u
            res = uid >> (nb.bit_length() - 1)
            i = uid & (nb - 1)
            q0 = pl.multiple_of(i * blk, blk)
            if chained:
                k0 = pl.multiple_of(jnp.maximum(q0 - blk, 0), blk)
                bias = jnp.where(i == 0, bias_lead, bias_window)
            else:
                k0 = pl.multiple_of(jnp.minimum(q0, sub_len - nk), blk)
                bias = jnp.where(i == nb - 1, bias_trail, bias_lead)
            unit(res, q0, k0, bias)
        return carry

    lax.fori_loop(0, (n_res * nb) // DIL_UNROLL, body, 0)


def _dil_group(zq, n_res, chained, name):
    batch, dil, sub_len, _ = zq.shape
    assert sub_len >= 2 * DIL_BLOCK

    def zspec(blk_idx):
        return pl.BlockSpec((None, n_res, sub_len, DIL_GW), lambda b, r: (b, r, 0, blk_idx))

    return pl.pallas_call(
        functools.partial(_dil_kernel, n_res=n_res, sub_len=sub_len, chained=chained),
        grid=(batch, dil // n_res),
        in_specs=[zspec(0), zspec(1), zspec(2)],
        out_specs=[pl.BlockSpec((None, n_res, sub_len, DIL_GW), lambda b, r: (b, r, 0, 0)),
                   pl.BlockSpec((None, n_res, sub_len, LANES), lambda b, r: (b, r, 0, 0))],
        out_shape=[jax.ShapeDtypeStruct((batch, dil, sub_len, DIL_GW), BF16),
                   jax.ShapeDtypeStruct((batch, dil, sub_len, LANES), F32)],
        compiler_params=_cparams(40),
        name=name,
    )(zq, zq, zq)


MERGE_TM = 512


def _shifted(prev_rows, cur, tm):
    full = jnp.concatenate([prev_rows, cur], axis=0)
    d1 = pltpu.roll(full, 1, 0)[SUBLANES:SUBLANES + tm]
    d2 = pltpu.roll(full, 2, 0)[SUBLANES:SUBLANES + tm]
    return d1, d2


def _merge_kernel(x_ref, scb_ref, scc_ref, scx_ref, ygla_ref, o0_ref, o1_ref, o2_ref,
                  l0_ref, l1_ref, l2_ref, gpre_ref, wg_ref, bg_ref, wbr_ref, wmix_ref,
                  gpost_ref, cw_ref, out_ref, halo_ref, oi1_ref, oi2_ref, li1_ref, li2_ref):
    tm = MERGE_TM

    @pl.when(pl.program_id(1) == 0)
    def _():
        halo_ref[...] = jnp.zeros(halo_ref.shape, F32)

    x = x_ref[...]
    h = _rms(x, gpre_ref[...]).astype(BF16)

    u = scc_ref[...].astype(F32) * scx_ref[...].astype(F32)
    u1, u2 = _shifted(halo_ref[...], u, tm)
    halo_ref[...] = u[tm - SUBLANES:, :]
    cw = cw_ref[...]
    y_sc = scb_ref[...].astype(F32) * (cw[0:1, :] * u2 + cw[1:2, :] * u1 + cw[2:3, :] * u)

    for (o_ref, l_ref, oi_ref, li_ref), (_, dil) in zip(
            ((o1_ref, l1_ref, oi1_ref, li1_ref), (o2_ref, l2_ref, oi2_ref, li2_ref)), DIL_PATTERNS[1:]):
        n = tm // dil
        for r in range(dil):
            o_r = o_ref[r].astype(F32)
            for hh in range(DIL_HEADS):
                oi_ref[hh, pl.ds(r, n, stride=dil), :] = o_r[:, hh * DIL_HD:(hh + 1) * DIL_HD]
            li_ref[pl.ds(r, n, stride=dil), :] = l_ref[r]

    stats = (l0_ref[0], li1_ref[...], li2_ref[...])
    lanes_per_head = LANES // DIL_HEADS
    parts = []
    for hh in range(DIL_HEADS):
        sl = slice(hh * DIL_HD, (hh + 1) * DIL_HD)
        lm = hh * lanes_per_head
        ld = lm + lanes_per_head // 2
        ma, mb, mc = (v[:, lm:lm + 1] for v in stats)
        da, db, dc = (v[:, ld:ld + 1] for v in stats)
        m = jnp.maximum(jnp.maximum(ma, mb), mc)
        ea, eb, ec = jnp.exp(ma - m), jnp.exp(mb - m), jnp.exp(mc - m)
        inv = 1.0 / (ea * da + eb * db + ec * dc)
        parts.append((ea * inv) * o0_ref[0, :, sl].astype(F32)
                     + (eb * inv) * oi1_ref[hh] + (ec * inv) * oi2_ref[hh])
    y_dil = jnp.concatenate(parts, axis=1)

    merged = jnp.zeros((tm, D_MODEL), F32)
    branches = (ygla_ref[...], y_sc.astype(BF16), y_dil.astype(BF16))
    for g, br in enumerate(branches):
        cs = slice(g * D_MODEL, (g + 1) * D_MODEL)
        gate = _sigmoid(_dot(h, wg_ref[:, cs]) + bg_ref[:, cs])
        merged = merged + gate * _dot(br, wbr_ref[g])
    mix = _dot(merged.astype(BF16), wmix_ref[...])
    out_ref[...] = x + _rms(mix, gpost_ref[...])


def _merge(x3, zn, ygla, o_list, l_list, gpre, wg, bg, wbr, wmix, gpost, cw):
    batch, seq, d = x3.shape
    tm = MERGE_TM

    def rows(width, blk_idx=0):
        return pl.BlockSpec((None, tm, width), lambda b, i: (b, i, blk_idx))

    def grouped(dil, width):
        return pl.BlockSpec((None, dil, tm // dil, width), lambda b, i: (b, 0, i, 0))

    dils = [dil for _, dil in DIL_PATTERNS]
    return pl.pallas_call(
        _merge_kernel,
        grid=(batch, seq // tm),
        in_specs=[rows(d),
                  rows(BRANCH_WIDTH, ZB_SCB), rows(BRANCH_WIDTH, ZB_SCC), rows(BRANCH_WIDTH, ZB_SCX),
                  rows(GLA_V)]
                 + [grouped(dil, DIL_GW) for dil in dils]
                 + [grouped(dil, LANES) for dil in dils]
                 + [_const_spec((1, d)),
                    _const_spec((d, 3 * d)),
                    _const_spec((1, 3 * d)),
                    _const_spec((3, BRANCH_WIDTH, d)),
                    _const_spec((d, d)),
                    _const_spec((1, d)),
                    _const_spec((CONV_K, BRANCH_WIDTH))],
        out_specs=rows(d),
        out_shape=jax.ShapeDtypeStruct((batch, seq, d), F32),
        scratch_shapes=[pltpu.VMEM((SUBLANES, BRANCH_WIDTH), F32),
                        pltpu.VMEM((DIL_HEADS, tm, DIL_HD), F32), pltpu.VMEM((DIL_HEADS, tm, DIL_HD), F32),
                        pltpu.VMEM((tm, LANES), F32), pltpu.VMEM((tm, LANES), F32)],
        compiler_params=_cparams(48, ("arbitrary", "arbitrary")),
        name="merge",
    )(x3, zn, zn, zn, ygla, *o_list, *l_list, gpre, wg, bg, wbr, wmix, gpost, cw)


FFN_TM = 512
FFN_CHUNKS = ((0, 1536), (1536, D_FF))


def _ffn_kernel(x_ref, g1_ref, wg_ref, wu_ref, cw_ref, cb_ref, wd_ref, g2_ref, out_ref, halo_ref):
    tm = FFN_TM

    @pl.when(pl.program_id(1) == 0)
    def _():
        halo_ref[...] = jnp.zeros(halo_ref.shape, F32)

    x = x_ref[...]
    h = _rms(x, g1_ref[...]).astype(BF16)
    acc = jnp.zeros((tm, D_MODEL), F32)
    for c0, c1 in FFN_CHUNKS:
        a = _dot(h, wg_ref[:, c0:c1])
        a1, a2 = _shifted(halo_ref[:, c0:c1], a, tm)
        halo_ref[:, c0:c1] = a[tm - SUBLANES:, :]
        cw = cw_ref[:, c0:c1]
        gt = cw[0:1, :] * a2 + cw[1:2, :] * a1 + cw[2:3, :] * a + cb_ref[:, c0:c1]
        ge = 0.5 * gt * (1.0 + jnp.tanh(0.7978845608028654 * (gt + 0.044715 * (gt * gt * gt))))
        up = _dot(h, wu_ref[:, c0:c1])
        acc = acc + _dot((ge * up).astype(BF16), wd_ref[c0:c1, :])
    out_ref[...] = x + _rms(acc, g2_ref[...])


def _ffn(x3, g1, wg, wu, cw, cb, wd, g2):
    batch, seq, d = x3.shape
    tm = FFN_TM
    xspec = pl.BlockSpec((None, tm, d), lambda b, i: (b, i, 0))
    return pl.pallas_call(
        _ffn_kernel,
        grid=(batch, seq // tm),
        in_specs=[xspec,
                  _const_spec((1, d)),
                  _const_spec((d, D_FF)),
                  _const_spec((d, D_FF)),
                  _const_spec((CONV_K, D_FF)),
                  _const_spec((1, D_FF)),
                  _const_spec((D_FF, d)),
                  _const_spec((1, d))],
        out_specs=xspec,
        out_shape=jax.ShapeDtypeStruct((batch, seq, d), F32),
        scratch_shapes=[pltpu.VMEM((SUBLANES, D_FF), F32)],
        compiler_params=_cparams(52, ("arbitrary", "arbitrary")),
        name="ffn",
    )(x3, g1, wg, wu, cw, cb, wd, g2)


def kernel(x, positions, w_in, w_alpha_up, b_alpha, gla_norm_g, sc_conv_w, w_gate, b_gate, w_branch,
           w_mix_out, pre_mix_g, post_mix_g, pre_ffn_g, post_ffn_g, w_ff_gate, w_ff_up, ff_conv_w,
           ff_conv_b, w_ff_down):
    batch, seq, d = x.shape
    depth = w_in.shape[0]
    tables = _rope_tables(positions)
    w_perm = _permute_w_in(w_in)
    rmat = jnp.asarray(_band_matrices(), BF16)

    for l in range(depth):
        zn, z0, z1, z2 = _inproj(x, pre_mix_g[l].reshape(1, d), w_perm, l, tables)
        wup = jnp.concatenate(
            [w_alpha_up[l], jnp.zeros((ALOW_PAD - GLA_RANK, GLA_QK), F32)], axis=0).astype(BF16)
        ygla = _gla(zn, wup, b_alpha[l].reshape(1, GLA_QK), gla_norm_g[l].reshape(1, GLA_DV), rmat)
        o_list, l_list = [], []
        for zq, n_res, (_, dil) in zip((z0, z1, z2), DIL_RES_PER_STEP, DIL_PATTERNS):
            sub_len = seq // dil
            if sub_len == DIL_BLOCK:
                o, lse = _dil_group(zq.reshape(batch, 1, seq, ZG_COLS), 1, False, f"dil_attn_d{dil}")
                o, lse = o.reshape(batch, dil, sub_len, DIL_GW), lse.reshape(batch, dil, sub_len, LANES)
            else:
                o, lse = _dil_group(zq, n_res, True, f"dil_attn_d{dil}")
            o_list.append(o)
            l_list.append(lse)
        x = _merge(x, zn, ygla, o_list, l_list, pre_mix_g[l].reshape(1, d),
                   w_gate[l].astype(BF16), b_gate[l].reshape(1, 3 * d), w_branch[l].astype(BF16),
                   w_mix_out[l].astype(BF16), post_mix_g[l].reshape(1, d), sc_conv_w[l])
        x = _ffn(x, pre_ffn_g[l].reshape(1, d), w_ff_gate[l].astype(BF16), w_ff_up[l].astype(BF16),
                 ff_conv_w[l], ff_conv_b[l].reshape(1, D_FF), w_ff_down[l].astype(BF16),
                 post_ffn_g[l].reshape(1, d))
    return x
```

```python
import functools

import jax
import jax.numpy as jnp
import numpy as np
from jax import lax
from jax.experimental import pallas as pl
from jax.experimental.pallas import tpu as pltpu

F32 = jnp.float32
BF16 = jnp.bfloat16

D_MODEL = 1024
EPS = 1e-6
BRANCH_WIDTH = 512
GLA_HEADS = 4
GLA_DV = 128
GLA_DK = 64
GLA_RANK = 16
GLA_TAU = 16.0
GLA_CHUNK = 16
GLA_QK = GLA_HEADS * GLA_DK
GLA_V = GLA_HEADS * GLA_DV
CONV_K = 3
DIL_PATTERNS = ((128, 1), (512, 4), (2048, 16))
DIL_GROUPS = len(DIL_PATTERNS)
DIL_HEADS = 4
DIL_HD = 128
DIL_GW = DIL_HEADS * DIL_HD
DIL_BLOCK = 128
ROPE_THETA = 10000.0
D_FF = 2816

LANES = 128
SUBLANES = 8
MIB = 1024 * 1024

ALOW_PAD = LANES
ZN_COLS = 2 * GLA_V + 3 * BRANCH_WIDTH + 2 * GLA_QK
ZB_GV, ZB_GO, ZB_SCB, ZB_SCC, ZB_SCX = 0, 1, 2, 3, 4
ZB_GQ, ZB_GK = 10, 11
ZG_COLS = 3 * DIL_GW
W_ALOW = DIL_GROUPS * ZG_COLS + ZN_COLS
W_COLS = W_ALOW + ALOW_PAD

NEG_BIG = -1e30


def _cparams(vmem_mib, sem=None):
    return pltpu.CompilerParams(dimension_semantics=sem, vmem_limit_bytes=int(vmem_mib * MIB))


def _const_spec(shape):
    nd = len(shape)
    return pl.BlockSpec(shape, lambda *_: (0,) * nd, pipeline_mode=pl.Buffered(1))


def _rms(xf, g):
    r = lax.rsqrt(jnp.mean(xf * xf, axis=-1, keepdims=True) + EPS)
    return xf * r * g


def _sigmoid(v):
    return 1.0 / (1.0 + jnp.exp(-v))


def _dot(a, b):
    return jnp.dot(a, b, preferred_element_type=F32)


def _dot_nt(a, b):
    return lax.dot_general(a, b, (((1,), (1,)), ((), ())), preferred_element_type=F32)


def _dot_tn(a, b):
    return lax.dot_general(a, b, (((0,), (0,)), ((), ())), preferred_element_type=F32)


def _rope_kernel(pos_ref, inv_ref, sgn_ref, c0_ref, s0_ref, c1_ref, s1_ref, c2_ref, s2_ref, *, seq):
    half = seq // 2
    lane = lax.broadcasted_iota(jnp.int32, (half, DIL_HD), 1)
    low = lane < DIL_HD // 2
    pos = jnp.where(low, pos_ref[0:half, :], pos_ref[half:seq, :]).astype(F32)
    ang = pos * inv_ref[...]
    cs, sn = jnp.cos(ang), jnp.sin(ang)
    cs_sw, sn_sw = pltpu.roll(cs, DIL_HD // 2, 1), pltpu.roll(sn, DIL_HD // 2, 1)
    c0_ref[0, 0:half, :] = jnp.where(low, cs, cs_sw)
    c0_ref[0, half:seq, :] = jnp.where(low, cs_sw, cs)
    s0_ref[0, 0:half, :] = jnp.where(low, sn, sn_sw) * sgn_ref[...]
    s0_ref[0, half:seq, :] = jnp.where(low, sn_sw, sn) * sgn_ref[...]
    for (c_ref, s_ref), (_, dil) in zip(((c1_ref, s1_ref), (c2_ref, s2_ref)), DIL_PATTERNS[1:]):
        n = seq // dil
        for r in range(dil):
            c_ref[r] = c0_ref[0, pl.ds(r, n, stride=dil), :]
            s_ref[r] = s0_ref[0, pl.ds(r, n, stride=dil), :]


def _rope_tables(positions):
    batch, seq = positions.shape
    inv = ROPE_THETA ** (-jnp.arange(0, DIL_HD, 2, dtype=F32) / DIL_HD)
    inv2 = jnp.concatenate([inv, inv]).reshape(1, DIL_HD)
    half = DIL_HD // 2
    sgn = jnp.concatenate([-jnp.ones((half,), F32), jnp.ones((half,), F32)]).reshape(1, DIL_HD)
    shapes, specs = [], []
    for _, dil in DIL_PATTERNS:
        shp = (batch, dil, seq // dil, DIL_HD)
        for _ in range(2):
            shapes.append(jax.ShapeDtypeStruct(shp, F32))
            specs.append(pl.BlockSpec((None,) + shp[1:], lambda b: (b, 0, 0, 0)))
    outs = pl.pallas_call(
        functools.partial(_rope_kernel, seq=seq),
        grid=(batch,),
        in_specs=[pl.BlockSpec((None, seq, 1), lambda b: (b, 0, 0)),
                  pl.BlockSpec((1, DIL_HD), lambda b: (0, 0)),
                  pl.BlockSpec((1, DIL_HD), lambda b: (0, 0))],
        out_specs=specs,
        out_shape=shapes,
        compiler_params=_cparams(32),
        name="rope_tables",
    )(positions.reshape(batch, seq, 1), inv2, sgn)
    return [(outs[2 * g], outs[2 * g + 1]) for g in range(DIL_GROUPS)]


_W_IN_SEGMENTS = tuple((o + gi * DIL_GW, DIL_GW) for gi in range(DIL_GROUPS) for o in (3088, 4624, 6160)) + (
    (512, 2 * GLA_V), (1552, 3 * BRANCH_WIDTH), (0, 2 * GLA_QK), (1536, GLA_RANK))
W_PREP_LANES = 128


def _w_in_kernel(w_ref, o_ref):
    dst = 0
    for src, width in _W_IN_SEGMENTS:
        o_ref[dst:dst + width, :] = w_ref[src:src + width, :].astype(BF16)
        dst += width
    o_ref[dst:W_COLS, :] = jnp.zeros((W_COLS - dst, o_ref.shape[1]), BF16)


def _permute_w_in(w_in):
    depth, d, cols = w_in.shape
    return pl.pallas_call(
        _w_in_kernel,
        grid=(depth, d // W_PREP_LANES),
        in_specs=[pl.BlockSpec((None, cols, W_PREP_LANES), lambda l, i: (l, 0, i))],
        out_specs=pl.BlockSpec((None, W_COLS, W_PREP_LANES), lambda l, i: (l, 0, i)),
        out_shape=jax.ShapeDtypeStruct((depth, W_COLS, d), BF16),
        compiler_params=_cparams(32),
        name="w_in_permute",
    )(jnp.swapaxes(w_in, 1, 2))


INPROJ_TM = 512
INPROJ_CH = 1024


def _inproj_kernel(x_ref, g_ref, w_ref, c0_ref, s0_ref, c1_ref, s1_ref, c2_ref, s2_ref, wup_ref, ba_ref,
                   zn_ref, z0_ref, z1_ref, z2_ref, la_ref, h_ref):
    tm = INPROJ_TM
    nlt = D_MODEL // LANES
    hf = _rms(x_ref[...], g_ref[...])
    for j in range(nlt):
        h_ref[j] = hf[:, j * LANES:(j + 1) * LANES]
    h = hf.astype(BF16)

    groups = ((z0_ref, c0_ref, s0_ref), (z1_ref, c1_ref, s1_ref), (z2_ref, c2_ref, s2_ref))
    for gi, (z_ref, c_ref, s_ref) in enumerate(groups):
        dil = DIL_PATTERNS[gi][1]
        n = tm // dil
        if dil == 1:
            hp = h
        else:
            hp = jnp.concatenate(
                [jnp.concatenate([h_ref[j, pl.ds(r, n, stride=dil), :] for j in range(nlt)], axis=1)
                 for r in range(dil)], axis=0).astype(BF16)
        cs = jnp.concatenate([c_ref[r] for r in range(dil)], axis=0)
        sn = jnp.concatenate([s_ref[r] for r in range(dil)], axis=0)
        for part in range(3):
            w0 = gi * ZG_COLS + part * DIL_GW
            res = _dot_nt(hp, w_ref[w0:w0 + DIL_GW, :])
            if part < 2:
                sc = DIL_HD ** -0.5 if part == 0 else 1.0
                heads = []
                for hh in range(DIL_HEADS):
                    xh = res[:, hh * DIL_HD:(hh + 1) * DIL_HD]
                    heads.append(xh * (cs * sc) + pltpu.roll(xh, DIL_HD // 2, 1) * (sn * sc))
                res = jnp.concatenate(heads, axis=1)
            resb = res.astype(BF16)
            for r in range(dil):
                z_ref[r, :, part * DIL_GW:(part + 1) * DIL_GW] = resb[r * n:(r + 1) * n, :]

    wn = DIL_GROUPS * ZG_COLS
    for c0 in range(0, ZN_COLS, INPROJ_CH):
        c1 = min(c0 + INPROJ_CH, ZN_COLS)
        zn_ref[:, c0:c1] = _dot_nt(h, w_ref[wn + c0:wn + c1, :]).astype(BF16)

    a_low = _dot_nt(h, w_ref[W_ALOW:W_COLS, :]).astype(BF16)
    xa = _dot(a_low, wup_ref[...]) + ba_ref[...]
    la_ref[...] = (jnp.minimum(xa, 0.0) - jnp.log(1.0 + jnp.exp(-jnp.abs(xa)))) * (1.0 / GLA_TAU)


def _inproj(x3, g, w_all, layer, tables, wup, ba):
    batch, seq, d = x3.shape
    tm = INPROJ_TM
    out_shapes = [jax.ShapeDtypeStruct((batch, seq, ZN_COLS), BF16)]
    out_specs = [pl.BlockSpec((None, tm, ZN_COLS), lambda b, i: (b, i, 0))]
    tab_specs, tab_args = [], []
    for (_, dil), (cos, sin) in zip(DIL_PATTERNS, tables):
        out_shapes.append(jax.ShapeDtypeStruct((batch, dil, seq // dil, ZG_COLS), BF16))
        out_specs.append(pl.BlockSpec((None, dil, tm // dil, ZG_COLS), lambda b, i: (b, 0, i, 0)))
        tab_specs += [pl.BlockSpec((None, dil, tm // dil, DIL_HD), lambda b, i: (b, 0, i, 0))] * 2
        tab_args += [cos, sin]
    out_shapes.append(jax.ShapeDtypeStruct((batch, seq, GLA_QK), F32))
    out_specs.append(pl.BlockSpec((None, tm, GLA_QK), lambda b, i: (b, i, 0)))
    return pl.pallas_call(
        _inproj_kernel,
        grid=(batch, seq // tm),
        in_specs=[pl.BlockSpec((None, tm, d), lambda b, i: (b, i, 0)),
                  _const_spec((1, d)),
                  pl.BlockSpec((None, W_COLS, d), lambda b, i: (layer, 0, 0),
                               pipeline_mode=pl.Buffered(1))] + tab_specs
                 + [_const_spec((ALOW_PAD, GLA_QK)), _const_spec((1, GLA_QK))],
        out_specs=out_specs,
        out_shape=out_shapes,
        scratch_shapes=[pltpu.VMEM((d // LANES, tm, LANES), F32)],
        compiler_params=_cparams(52),
        name="inproj",
    )(x3, g, w_all, *tab_args, wup, ba)


GLA_TS = 128
GLA_NCH = GLA_TS // GLA_CHUNK
GLA_FAST_SPAN = 60.0
GLA_FAST_TS = 256
GLA_FAST_UNROLL = 2
GLA_SEQ_PER_STEP = 2


def _band_col(h, r):
    base = (h // 2) * LANES
    if h % 2 == 0:
        return base + (LANES - r) % LANES
    return base + 2 * GLA_CHUNK - r


def _band_matrices():
    m = np.zeros((GLA_CHUNK, GLA_QK, 2 * LANES), np.float32)
    for r in range(GLA_CHUNK):
        for h in range(GLA_HEADS):
            m[r, h * GLA_DK:(h + 1) * GLA_DK, _band_col(h, r)] = 1.0
    return m


def _gla_kernel(q_ref, k_ref, v_ref, go_ref, la_ref, ng_ref, rmat_ref, y_ref, s_ref, st_ref, *, seq):
    ts, c, nch, nseq = GLA_TS, GLA_CHUNK, GLA_NCH, GLA_SEQ_PER_STEP
    s_ref[...] = jnp.zeros(s_ref.shape, F32)
    st_ref[...] = jnp.zeros(st_ref.shape, F32)

    def span_tile(ti, worst):
        r0 = pl.multiple_of(ti * ts, ts)
        for b in range(nseq):
            worst = jnp.maximum(worst, -jnp.sum(la_ref[b, pl.ds(r0, ts), :], axis=0, keepdims=True))
        return worst

    worst = lax.fori_loop(0, seq // ts, span_tile, jnp.zeros((1, GLA_QK), F32))
    decay_span = jnp.max(worst)

    row = lax.broadcasted_iota(jnp.int32, (ts, ts), 0)
    col = lax.broadcasted_iota(jnp.int32, (ts, ts), 1)
    same = (row >> 4) == (col >> 4)
    tri = jnp.where(same, jnp.where(col <= row, 1.0, 0.0), 0.0).astype(F32)
    blk = jnp.where(same, 1.0, 0.0).astype(F32)
    rowmod = lax.broadcasted_iota(jnp.int32, (ts, GLA_QK), 0) & (c - 1)
    keep_even = (col == 0) | (col > LANES - c)
    keep_odd = (col > c) & (col <= 2 * c)
    chunk_of_col = lax.broadcasted_iota(jnp.int32, (GLA_DK, ts), 1) >> 4
    q8_row = lax.broadcasted_iota(jnp.int32, (ts, nch * GLA_DK), 0) >> 4
    q8_col = lax.broadcasted_iota(jnp.int32, (ts, nch * GLA_DK), 1) >> 6

    def finish(o, b, r0, h, rows):
        on = o * lax.rsqrt(jnp.mean(o * o, axis=-1, keepdims=True) + EPS) * ng_ref[...]
        g = go_ref[b, pl.ds(r0, rows), h * GLA_DV:(h + 1) * GLA_DV].astype(F32)
        y_ref[b, pl.ds(r0, rows), h * GLA_DV:(h + 1) * GLA_DV] = (on * g * _sigmoid(g)).astype(BF16)

    def safe_tile(b, r0):
        q = q_ref[b, pl.ds(r0, ts), :].astype(F32) * (GLA_DK ** -0.5)
        k = k_ref[b, pl.ds(r0, ts), :].astype(F32)
        loga = la_ref[b, pl.ds(r0, ts), :]
        lc = jnp.dot(tri, loga, precision=lax.Precision.HIGHEST, preferred_element_type=F32)
        lend = jnp.dot(blk, loga, precision=lax.Precision.HIGHEST, preferred_element_type=F32)
        q_in = q * jnp.exp(lc)
        k_in = k * jnp.exp(lend - lc)

        band = _dot((q * k).astype(BF16), rmat_ref[0])
        rel = jnp.zeros_like(loga)
        for r in range(1, c):
            rel = rel + (loga if r == 1 else pltpu.roll(loga, r - 1, 0))
            term = jnp.where(rowmod >= r, q * pltpu.roll(k, r, 0) * jnp.exp(rel), 0.0)
            band = band + _dot(term.astype(BF16), rmat_ref[r])

        kin_t = k_in.T
        lend_t = lend.T
        for h in range(GLA_HEADS):
            vh = v_ref[b, pl.ds(r0, ts), h * GLA_DV:(h + 1) * GLA_DV]
            bt = band[:, (h // 2) * LANES:(h // 2 + 1) * LANES]
            if h % 2 == 0:
                p = pltpu.roll(jnp.where(keep_even, bt, 0.0), 0, 1, stride=1, stride_axis=0)
            else:
                p = pltpu.roll(jnp.where(keep_odd, bt, 0.0), LANES - 2 * c, 1, stride=1, stride_axis=0)
            o = _dot(p.astype(BF16), vh)

            kt = kin_t[h * GLA_DK:(h + 1) * GLA_DK, :]
            kst = jnp.concatenate(
                [jnp.where(chunk_of_col == n, kt, 0.0) for n in range(nch)], axis=0).astype(BF16)
            u = _dot(kst, vh)
            lt = lend_t[h * GLA_DK:(h + 1) * GLA_DK, :]
            st = s_ref[b * GLA_HEADS + h]
            states = []
            for n in range(nch):
                states.append(st)
                st = st * jnp.exp(lt[:, n * c:n * c + 1]) + u[n * GLA_DK:(n + 1) * GLA_DK, :]
            s_ref[b * GLA_HEADS + h] = st
            sst = jnp.concatenate(states, axis=0).astype(BF16)
            qh = q_in[:, h * GLA_DK:(h + 1) * GLA_DK]
            q8 = jnp.concatenate([qh] * nch, axis=1)
            qx = jnp.where(q8_row == q8_col, q8, 0.0).astype(BF16)
            finish(o + _dot(qx, sst), b, r0, h, ts)

    tf = GLA_FAST_TS
    causal = (lax.broadcasted_iota(jnp.int32, (tf, tf), 1) <= lax.broadcasted_iota(jnp.int32, (tf, tf), 0))
    tri_full = jnp.where(causal, 1.0, 0.0).astype(BF16)

    def fast_tile(b, r0):
        loga = la_ref[b, pl.ds(r0, tf), :]
        hi = loga.astype(BF16)
        rest = loga - hi.astype(F32)
        mid = rest.astype(BF16)
        lo = (rest - mid.astype(F32)).astype(BF16)
        cum = _dot(tri_full, hi) + _dot(tri_full, mid) + _dot(tri_full, lo)
        lend = cum[tf - 1:tf, :]
        dec = jnp.exp(lend)
        q = q_ref[b, pl.ds(r0, tf), :].astype(F32) * (GLA_DK ** -0.5)
        k = k_ref[b, pl.ds(r0, tf), :].astype(F32)
        qt = (q * jnp.exp(cum)).astype(BF16)
        kt = k * jnp.exp(-cum)
        ks = (kt * dec).astype(BF16)
        kt = kt.astype(BF16)
        for h in range(GLA_HEADS):
            ds = slice(h * GLA_DK, (h + 1) * GLA_DK)
            vh = v_ref[b, pl.ds(r0, tf), h * GLA_DV:(h + 1) * GLA_DV]
            p = jnp.where(causal, _dot_nt(qt[:, ds], kt[:, ds]), 0.0).astype(BF16)
            st = st_ref[b * GLA_HEADS + h]
            o = _dot(p, vh) + _dot_nt(qt[:, ds], st.astype(BF16))
            st_ref[b * GLA_HEADS + h] = st * dec[:, ds] + _dot_tn(vh, ks[:, ds])
            finish(o, b, r0, h, tf)

    def run(tile_fn, rows, unroll):
        def body(ti, carry):
            r0 = pl.multiple_of(ti * rows, rows)
            for b in range(nseq):
                tile_fn(b, r0)
            return carry
        lax.fori_loop(0, seq // rows, body, 0, unroll=unroll)

    fast_ok = decay_span * (tf // ts) <= GLA_FAST_SPAN

    @pl.when(fast_ok)
    def _():
        run(fast_tile, tf, GLA_FAST_UNROLL)

    @pl.when(jnp.logical_not(fast_ok))
    def _():
        run(safe_tile, ts, 1)


def _gla(zn, loga, ng, rmat):
    batch, seq, _ = zn.shape
    nseq = GLA_SEQ_PER_STEP
    assert batch % nseq == 0

    def zspec(width, blk_idx):
        return pl.BlockSpec((nseq, seq, width), lambda b: (b, 0, blk_idx))

    return pl.pallas_call(
        functools.partial(_gla_kernel, seq=seq),
        grid=(batch // nseq,),
        in_specs=[zspec(GLA_QK, ZB_GQ), zspec(GLA_QK, ZB_GK), zspec(GLA_V, ZB_GV),
                  zspec(GLA_V, ZB_GO), zspec(GLA_QK, 0),
                  _const_spec((1, GLA_DV)),
                  _const_spec((GLA_CHUNK, GLA_QK, 2 * LANES))],
        out_specs=pl.BlockSpec((nseq, seq, GLA_V), lambda b: (b, 0, 0)),
        out_shape=jax.ShapeDtypeStruct((batch, seq, GLA_V), BF16),
        scratch_shapes=[pltpu.VMEM((nseq * GLA_HEADS, GLA_DK, GLA_DV), F32),
                        pltpu.VMEM((nseq * GLA_HEADS, GLA_DV, GLA_DK), F32)],
        compiler_params=_cparams(48),
        name="gla",
    )(zn, zn, zn, zn, loga, ng, rmat)


DIL_UNROLL = 16
DIL_RES_PER_STEP = (1, 4, 1)


def _dil_kernel(q_ref, k_ref, v_ref, o_ref, st_ref, *, n_res, sub_len, chained):
    blk = DIL_BLOCK
    nb = sub_len // blk
    nk = 2 * blk
    qi = lax.broadcasted_iota(jnp.int32, (blk, nk), 0)
    km = lax.broadcasted_iota(jnp.int32, (blk, nk), 1)
    bias_window = jnp.where((km >= qi) & (km <= qi + blk), 0.0, NEG_BIG).astype(F32)
    bias_lead = jnp.where(km <= qi, 0.0, NEG_BIG).astype(F32)
    bias_trail = jnp.where((km >= blk) & (km <= qi + blk), 0.0, NEG_BIG).astype(F32)
    ones = jnp.ones((nk, DIL_HD), BF16)
    lane = lax.broadcasted_iota(jnp.int32, (blk, LANES), 1)
    lanes_per_head = LANES // DIL_HEADS
    max_lane = (lane & (lanes_per_head // 2)) == 0

    def unit(res, q0, k0, bias):
        stats = jnp.zeros((blk, LANES), F32)
        for h in range(DIL_HEADS):
            sl = slice(h * DIL_HD, (h + 1) * DIL_HD)
            q = q_ref[res, pl.ds(q0, blk), sl]
            kc = k_ref[res, pl.ds(k0, nk), sl]
            vc = jnp.concatenate([v_ref[res, pl.ds(k0, nk), sl], ones], axis=1)
            s = _dot_nt(q, kc) + bias
            mx = jnp.max(s, axis=-1, keepdims=True)
            p = jnp.exp(s - mx)
            ov = _dot(p.astype(BF16), vc)
            o_ref[res, pl.ds(q0, blk), sl] = ov[:, :DIL_HD].astype(BF16)
            stats = jnp.where(lane >= h * lanes_per_head, jnp.where(max_lane, mx, ov[:, DIL_HD:]), stats)
        st_ref[res, pl.ds(q0, blk), :] = stats

    assert (n_res * nb) % DIL_UNROLL == 0

    def body(g, carry):
        for u in range(DIL_UNROLL):
            uid = g * DIL_UNROLL + u
            res = uid >> (nb.bit_length() - 1)
            i = uid & (nb - 1)
            q0 = pl.multiple_of(i * blk, blk)
            if chained:
                k0 = pl.multiple_of(jnp.maximum(q0 - blk, 0), blk)
                bias = jnp.where(i == 0, bias_lead, bias_window)
            else:
                k0 = pl.multiple_of(jnp.minimum(q0, sub_len - nk), blk)
                bias = jnp.where(i == nb - 1, bias_trail, bias_lead)
            unit(res, q0, k0, bias)
        return carry

    lax.fori_loop(0, (n_res * nb) // DIL_UNROLL, body, 0)


def _dil_group(zq, n_res, chained, name):
    batch, dil, sub_len, _ = zq.shape
    assert sub_len >= 2 * DIL_BLOCK

    def zspec(blk_idx):
        return pl.BlockSpec((None, n_res, sub_len, DIL_GW), lambda b, r: (b, r, 0, blk_idx))

    return pl.pallas_call(
        functools.partial(_dil_kernel, n_res=n_res, sub_len=sub_len, chained=chained),
        grid=(batch, dil // n_res),
        in_specs=[zspec(0), zspec(1), zspec(2)],
        out_specs=[pl.BlockSpec((None, n_res, sub_len, DIL_GW), lambda b, r: (b, r, 0, 0)),
                   pl.BlockSpec((None, n_res, sub_len, LANES), lambda b, r: (b, r, 0, 0))],
        out_shape=[jax.ShapeDtypeStruct((batch, dil, sub_len, DIL_GW), BF16),
                   jax.ShapeDtypeStruct((batch, dil, sub_len, LANES), F32)],
        compiler_params=_cparams(40),
        name=name,
    )(zq, zq, zq)


MERGE_TM = 512


def _shifted(prev_rows, cur, tm):
    full = jnp.concatenate([prev_rows, cur], axis=0)
    d1 = pltpu.roll(full, 1, 0)[SUBLANES:SUBLANES + tm]
    d2 = pltpu.roll(full, 2, 0)[SUBLANES:SUBLANES + tm]
    return d1, d2


def _merge_kernel(x_ref, scb_ref, scc_ref, scx_ref, ygla_ref, o0_ref, o1_ref, o2_ref,
                  l0_ref, l1_ref, l2_ref, gpre_ref, wg_ref, bg_ref, wbr_ref, wmix_ref,
                  gpost_ref, cw_ref, out_ref, halo_ref, oi1_ref, oi2_ref, li1_ref, li2_ref):
    tm = MERGE_TM

    @pl.when(pl.program_id(1) == 0)
    def _():
        halo_ref[...] = jnp.zeros(halo_ref.shape, F32)

    x = x_ref[...]
    h = _rms(x, gpre_ref[...]).astype(BF16)

    u = scc_ref[...].astype(F32) * scx_ref[...].astype(F32)
    u1, u2 = _shifted(halo_ref[...], u, tm)
    halo_ref[...] = u[tm - SUBLANES:, :]
    cw = cw_ref[...]
    y_sc = scb_ref[...].astype(F32) * (cw[0:1, :] * u2 + cw[1:2, :] * u1 + cw[2:3, :] * u)

    for (o_ref, l_ref, oi_ref, li_ref), (_, dil) in zip(
            ((o1_ref, l1_ref, oi1_ref, li1_ref), (o2_ref, l2_ref, oi2_ref, li2_ref)), DIL_PATTERNS[1:]):
        n = tm // dil
        for r in range(dil):
            o_r = o_ref[r].astype(F32)
            for hh in range(DIL_HEADS):
                oi_ref[hh, pl.ds(r, n, stride=dil), :] = o_r[:, hh * DIL_HD:(hh + 1) * DIL_HD]
            li_ref[pl.ds(r, n, stride=dil), :] = l_ref[r]

    stats = (l0_ref[0], li1_ref[...], li2_ref[...])
    lanes_per_head = LANES // DIL_HEADS
    parts = []
    for hh in range(DIL_HEADS):
        sl = slice(hh * DIL_HD, (hh + 1) * DIL_HD)
        lm = hh * lanes_per_head
        ld = lm + lanes_per_head // 2
        ma, mb, mc = (v[:, lm:lm + 1] for v in stats)
        da, db, dc = (v[:, ld:ld + 1] for v in stats)
        m = jnp.maximum(jnp.maximum(ma, mb), mc)
        ea, eb, ec = jnp.exp(ma - m), jnp.exp(mb - m), jnp.exp(mc - m)
        inv = 1.0 / (ea * da + eb * db + ec * dc)
        parts.append((ea * inv) * o0_ref[0, :, sl].astype(F32)
                     + (eb * inv) * oi1_ref[hh] + (ec * inv) * oi2_ref[hh])
    y_dil = jnp.concatenate(parts, axis=1)

    merged = jnp.zeros((tm, D_MODEL), F32)
    branches = (ygla_ref[...], y_sc.astype(BF16), y_dil.astype(BF16))
    for g, br in enumerate(branches):
        cs = slice(g * D_MODEL, (g + 1) * D_MODEL)
        gate = _sigmoid(_dot(h, wg_ref[:, cs]) + bg_ref[:, cs])
        merged = merged + gate * _dot(br, wbr_ref[g])
    mix = _dot(merged.astype(BF16), wmix_ref[...])
    out_ref[...] = x + _rms(mix, gpost_ref[...])


def _merge(x3, zn, ygla, o_list, l_list, gpre, wg, bg, wbr, wmix, gpost, cw):
    batch, seq, d = x3.shape
    tm = MERGE_TM

    def rows(width, blk_idx=0):
        return pl.BlockSpec((None, tm, width), lambda b, i: (b, i, blk_idx))

    def grouped(dil, width):
        return pl.BlockSpec((None, dil, tm // dil, width), lambda b, i: (b, 0, i, 0))

    dils = [dil for _, dil in DIL_PATTERNS]
    return pl.pallas_call(
        _merge_kernel,
        grid=(batch, seq // tm),
        in_specs=[rows(d),
                  rows(BRANCH_WIDTH, ZB_SCB), rows(BRANCH_WIDTH, ZB_SCC), rows(BRANCH_WIDTH, ZB_SCX),
                  rows(GLA_V)]
                 + [grouped(dil, DIL_GW) for dil in dils]
                 + [grouped(dil, LANES) for dil in dils]
                 + [_const_spec((1, d)),
                    _const_spec((d, 3 * d)),
                    _const_spec((1, 3 * d)),
                    _const_spec((3, BRANCH_WIDTH, d)),
                    _const_spec((d, d)),
                    _const_spec((1, d)),
                    _const_spec((CONV_K, BRANCH_WIDTH))],
        out_specs=rows(d),
        out_shape=jax.ShapeDtypeStruct((batch, seq, d), F32),
        scratch_shapes=[pltpu.VMEM((SUBLANES, BRANCH_WIDTH), F32),
                        pltpu.VMEM((DIL_HEADS, tm, DIL_HD), F32), pltpu.VMEM((DIL_HEADS, tm, DIL_HD), F32),
                        pltpu.VMEM((tm, LANES), F32), pltpu.VMEM((tm, LANES), F32)],
        compiler_params=_cparams(48, ("arbitrary", "arbitrary")),
        name="merge",
    )(x3, zn, zn, zn, ygla, *o_list, *l_list, gpre, wg, bg, wbr, wmix, gpost, cw)


FFN_TM = 512
FFN_CHUNKS = ((0, 1536), (1536, D_FF))


def _ffn_kernel(x_ref, g1_ref, wg_ref, wu_ref, cw_ref, cb_ref, wd_ref, g2_ref, out_ref, halo_ref):
    tm = FFN_TM

    @pl.when(pl.program_id(1) == 0)
    def _():
        halo_ref[...] = jnp.zeros(halo_ref.shape, F32)

    x = x_ref[...]
    h = _rms(x, g1_ref[...]).astype(BF16)
    acc = jnp.zeros((tm, D_MODEL), F32)
    for c0, c1 in FFN_CHUNKS:
        a = _dot(h, wg_ref[:, c0:c1])
        a1, a2 = _shifted(halo_ref[:, c0:c1], a, tm)
        halo_ref[:, c0:c1] = a[tm - SUBLANES:, :]
        cw = cw_ref[:, c0:c1]
        gt = cw[0:1, :] * a2 + cw[1:2, :] * a1 + cw[2:3, :] * a + cb_ref[:, c0:c1]
        ge = 0.5 * gt * (1.0 + jnp.tanh(0.7978845608028654 * (gt + 0.044715 * (gt * gt * gt))))
        up = _dot(h, wu_ref[:, c0:c1])
        acc = acc + _dot((ge * up).astype(BF16), wd_ref[c0:c1, :])
    out_ref[...] = x + _rms(acc, g2_ref[...])


def _ffn(x3, g1, wg, wu, cw, cb, wd, g2):
    batch, seq, d = x3.shape
    tm = FFN_TM
    xspec = pl.BlockSpec((None, tm, d), lambda b, i: (b, i, 0))
    return pl.pallas_call(
        _ffn_kernel,
        grid=(batch, seq // tm),
        in_specs=[xspec,
                  _const_spec((1, d)),
                  _const_spec((d, D_FF)),
                  _const_spec((d, D_FF)),
                  _const_spec((CONV_K, D_FF)),
                  _const_spec((1, D_FF)),
                  _const_spec((D_FF, d)),
                  _const_spec((1, d))],
        out_specs=xspec,
        out_shape=jax.ShapeDtypeStruct((batch, seq, d), F32),
        scratch_shapes=[pltpu.VMEM((SUBLANES, D_FF), F32)],
        compiler_params=_cparams(52, ("arbitrary", "arbitrary")),
        name="ffn",
    )(x3, g1, wg, wu, cw, cb, wd, g2)


def kernel(x, positions, w_in, w_alpha_up, b_alpha, gla_norm_g, sc_conv_w, w_gate, b_gate, w_branch,
           w_mix_out, pre_mix_g, post_mix_g, pre_ffn_g, post_ffn_g, w_ff_gate, w_ff_up, ff_conv_w,
           ff_conv_b, w_ff_down):
    batch, seq, d = x.shape
    depth = w_in.shape[0]
    tables = _rope_tables(positions)
    w_perm = _permute_w_in(w_in)
    rmat = jnp.asarray(_band_matrices(), BF16)

    for l in range(depth):
        wup = jnp.concatenate(
            [w_alpha_up[l], jnp.zeros((ALOW_PAD - GLA_RANK, GLA_QK), F32)], axis=0).astype(BF16)
        zn, z0, z1, z2, loga = _inproj(x, pre_mix_g[l].reshape(1, d), w_perm, l, tables,
                                       wup, b_alpha[l].reshape(1, GLA_QK))
        ygla = _gla(zn, loga, gla_norm_g[l].reshape(1, GLA_DV), rmat)
        o_list, l_list = [], []
        for zq, n_res, (_, dil) in zip((z0, z1, z2), DIL_RES_PER_STEP, DIL_PATTERNS):
            sub_len = seq // dil
            if sub_len == DIL_BLOCK:
                o, lse = _dil_group(zq.reshape(batch, 1, seq, ZG_COLS), 1, False, f"dil_attn_d{dil}")
                o, lse = o.reshape(batch, dil, sub_len, DIL_GW), lse.reshape(batch, dil, sub_len, LANES)
            else:
                o, lse = _dil_group(zq, n_res, True, f"dil_attn_d{dil}")
            o_list.append(o)
            l_list.append(lse)
        x = _merge(x, zn, ygla, o_list, l_list, pre_mix_g[l].reshape(1, d),
                   w_gate[l].astype(BF16), b_gate[l].reshape(1, 3 * d), w_branch[l].astype(BF16),
                   w_mix_out[l].astype(BF16), post_mix_g[l].reshape(1, d), sc_conv_w[l])
        x = _ffn(x, pre_ffn_g[l].reshape(1, d), w_ff_gate[l].astype(BF16), w_ff_up[l].astype(BF16),
                 ff_conv_w[l], ff_conv_b[l].reshape(1, D_FF), w_ff_down[l].astype(BF16),
                 post_ffn_g[l].reshape(1, d))
    return x
```

```python
import functools

import jax
import jax.numpy as jnp
import numpy as np
from jax import lax
from jax.experimental import pallas as pl
from jax.experimental.pallas import tpu as pltpu

F32 = jnp.float32
BF16 = jnp.bfloat16

D_MODEL = 1024
EPS = 1e-6
BRANCH_WIDTH = 512
GLA_HEADS = 4
GLA_DV = 128
GLA_DK = 64
GLA_RANK = 16
GLA_TAU = 16.0
GLA_CHUNK = 16
GLA_QK = GLA_HEADS * GLA_DK
GLA_V = GLA_HEADS * GLA_DV
CONV_K = 3
DIL_PATTERNS = ((128, 1), (512, 4), (2048, 16))
DIL_GROUPS = len(DIL_PATTERNS)
DIL_HEADS = 4
DIL_HD = 128
DIL_GW = DIL_HEADS * DIL_HD
DIL_BLOCK = 128
ROPE_THETA = 10000.0
D_FF = 2816

LANES = 128
SUBLANES = 8
MIB = 1024 * 1024

ALOW_PAD = LANES
ZN_COLS = 2 * GLA_V + 3 * BRANCH_WIDTH + 2 * GLA_QK
ZB_GV, ZB_GO, ZB_SCB, ZB_SCC, ZB_SCX = 0, 1, 2, 3, 4
ZB_GQ, ZB_GK = 10, 11
ZG_COLS = 3 * DIL_GW
W_ALOW = DIL_GROUPS * ZG_COLS + ZN_COLS
W_COLS = W_ALOW + ALOW_PAD

NEG_BIG = -1e30


def _cparams(vmem_mib, sem=None):
    return pltpu.CompilerParams(dimension_semantics=sem, vmem_limit_bytes=int(vmem_mib * MIB))


def _const_spec(shape):
    nd = len(shape)
    return pl.BlockSpec(shape, lambda *_: (0,) * nd, pipeline_mode=pl.Buffered(1))


def _rms(xf, g):
    r = lax.rsqrt(jnp.mean(xf * xf, axis=-1, keepdims=True) + EPS)
    return xf * r * g


def _sigmoid(v):
    return 1.0 / (1.0 + jnp.exp(-v))


def _dot(a, b):
    return jnp.dot(a, b, preferred_element_type=F32)


def _dot_nt(a, b):
    return lax.dot_general(a, b, (((1,), (1,)), ((), ())), preferred_element_type=F32)


def _dot_tn(a, b):
    return lax.dot_general(a, b, (((0,), (0,)), ((), ())), preferred_element_type=F32)


def _rope_kernel(pos_ref, inv_ref, sgn_ref, c0_ref, s0_ref, c1_ref, s1_ref, c2_ref, s2_ref, *, seq):
    half = seq // 2
    lane = lax.broadcasted_iota(jnp.int32, (half, DIL_HD), 1)
    low = lane < DIL_HD // 2
    pos = jnp.where(low, pos_ref[0:half, :], pos_ref[half:seq, :]).astype(F32)
    ang = pos * inv_ref[...]
    cs, sn = jnp.cos(ang), jnp.sin(ang)
    cs_sw, sn_sw = pltpu.roll(cs, DIL_HD // 2, 1), pltpu.roll(sn, DIL_HD // 2, 1)
    c0_ref[0, 0:half, :] = jnp.where(low, cs, cs_sw)
    c0_ref[0, half:seq, :] = jnp.where(low, cs_sw, cs)
    s0_ref[0, 0:half, :] = jnp.where(low, sn, sn_sw) * sgn_ref[...]
    s0_ref[0, half:seq, :] = jnp.where(low, sn_sw, sn) * sgn_ref[...]
    for (c_ref, s_ref), (_, dil) in zip(((c1_ref, s1_ref), (c2_ref, s2_ref)), DIL_PATTERNS[1:]):
        n = seq // dil
        for r in range(dil):
            c_ref[r] = c0_ref[0, pl.ds(r, n, stride=dil), :]
            s_ref[r] = s0_ref[0, pl.ds(r, n, stride=dil), :]


def _rope_tables(positions):
    batch, seq = positions.shape
    inv = ROPE_THETA ** (-jnp.arange(0, DIL_HD, 2, dtype=F32) / DIL_HD)
    inv2 = jnp.concatenate([inv, inv]).reshape(1, DIL_HD)
    half = DIL_HD // 2
    sgn = jnp.concatenate([-jnp.ones((half,), F32), jnp.ones((half,), F32)]).reshape(1, DIL_HD)
    shapes, specs = [], []
    for _, dil in DIL_PATTERNS:
        shp = (batch, dil, seq // dil, DIL_HD)
        for _ in range(2):
            shapes.append(jax.ShapeDtypeStruct(shp, F32))
            specs.append(pl.BlockSpec((None,) + shp[1:], lambda b: (b, 0, 0, 0)))
    outs = pl.pallas_call(
        functools.partial(_rope_kernel, seq=seq),
        grid=(batch,),
        in_specs=[pl.BlockSpec((None, seq, 1), lambda b: (b, 0, 0)),
                  pl.BlockSpec((1, DIL_HD), lambda b: (0, 0)),
                  pl.BlockSpec((1, DIL_HD), lambda b: (0, 0))],
        out_specs=specs,
        out_shape=shapes,
        compiler_params=_cparams(32),
        name="rope_tables",
    )(positions.reshape(batch, seq, 1), inv2, sgn)
    return [(outs[2 * g], outs[2 * g + 1]) for g in range(DIL_GROUPS)]


_W_IN_SEGMENTS = tuple((o + gi * DIL_GW, DIL_GW) for gi in range(DIL_GROUPS) for o in (3088, 4624, 6160)) + (
    (512, 2 * GLA_V), (1552, 3 * BRANCH_WIDTH), (0, 2 * GLA_QK), (1536, GLA_RANK))
W_PREP_LANES = 128


def _w_in_kernel(w_ref, o_ref):
    dst = 0
    for src, width in _W_IN_SEGMENTS:
        o_ref[dst:dst + width, :] = w_ref[src:src + width, :].astype(BF16)
        dst += width
    o_ref[dst:W_COLS, :] = jnp.zeros((W_COLS - dst, o_ref.shape[1]), BF16)


def _permute_w_in(w_in):
    depth, d, cols = w_in.shape
    return pl.pallas_call(
        _w_in_kernel,
        grid=(depth, d // W_PREP_LANES),
        in_specs=[pl.BlockSpec((None, cols, W_PREP_LANES), lambda l, i: (l, 0, i))],
        out_specs=pl.BlockSpec((None, W_COLS, W_PREP_LANES), lambda l, i: (l, 0, i)),
        out_shape=jax.ShapeDtypeStruct((depth, W_COLS, d), BF16),
        compiler_params=_cparams(32),
        name="w_in_permute",
    )(jnp.swapaxes(w_in, 1, 2))


INPROJ_TM = 512
INPROJ_CH = 1024


def _inproj_kernel(x_ref, g_ref, w_ref, c0_ref, s0_ref, c1_ref, s1_ref, c2_ref, s2_ref, wup_ref, ba_ref,
                   zn_ref, z0_ref, z1_ref, z2_ref, la_ref, h_ref):
    tm = INPROJ_TM
    nlt = D_MODEL // LANES
    hf = _rms(x_ref[...], g_ref[...])
    for j in range(nlt):
        h_ref[j] = hf[:, j * LANES:(j + 1) * LANES]
    h = hf.astype(BF16)

    groups = ((z0_ref, c0_ref, s0_ref), (z1_ref, c1_ref, s1_ref), (z2_ref, c2_ref, s2_ref))
    for gi, (z_ref, c_ref, s_ref) in enumerate(groups):
        dil = DIL_PATTERNS[gi][1]
        n = tm // dil
        if dil == 1:
            hp = h
        else:
            hp = jnp.concatenate(
                [jnp.concatenate([h_ref[j, pl.ds(r, n, stride=dil), :] for j in range(nlt)], axis=1)
                 for r in range(dil)], axis=0).astype(BF16)
        cs = jnp.concatenate([c_ref[r] for r in range(dil)], axis=0)
        sn = jnp.concatenate([s_ref[r] for r in range(dil)], axis=0)
        for part in range(3):
            w0 = gi * ZG_COLS + part * DIL_GW
            res = _dot_nt(hp, w_ref[w0:w0 + DIL_GW, :])
            if part < 2:
                sc = DIL_HD ** -0.5 if part == 0 else 1.0
                heads = []
                for hh in range(DIL_HEADS):
                    xh = res[:, hh * DIL_HD:(hh + 1) * DIL_HD]
                    heads.append(xh * (cs * sc) + pltpu.roll(xh, DIL_HD // 2, 1) * (sn * sc))
                res = jnp.concatenate(heads, axis=1)
            resb = res.astype(BF16)
            for r in range(dil):
                z_ref[part, r] = resb[r * n:(r + 1) * n, :]

    wn = DIL_GROUPS * ZG_COLS
    for c0 in range(0, ZN_COLS, INPROJ_CH):
        c1 = min(c0 + INPROJ_CH, ZN_COLS)
        zn_ref[:, c0:c1] = _dot_nt(h, w_ref[wn + c0:wn + c1, :]).astype(BF16)

    a_low = _dot_nt(h, w_ref[W_ALOW:W_COLS, :]).astype(BF16)
    xa = _dot(a_low, wup_ref[...]) + ba_ref[...]
    la_ref[...] = (jnp.minimum(xa, 0.0) - jnp.log(1.0 + jnp.exp(-jnp.abs(xa)))) * (1.0 / GLA_TAU)


def _inproj(x3, g, w_all, layer, tables, wup, ba):
    batch, seq, d = x3.shape
    tm = INPROJ_TM
    out_shapes = [jax.ShapeDtypeStruct((batch, seq, ZN_COLS), BF16)]
    out_specs = [pl.BlockSpec((None, tm, ZN_COLS), lambda b, i: (b, i, 0))]
    tab_specs, tab_args = [], []
    for (_, dil), (cos, sin) in zip(DIL_PATTERNS, tables):
        out_shapes.append(jax.ShapeDtypeStruct((3, batch, dil, seq // dil, DIL_GW), BF16))
        out_specs.append(pl.BlockSpec((3, None, dil, tm // dil, DIL_GW), lambda b, i: (0, b, 0, i, 0)))
        tab_specs += [pl.BlockSpec((None, dil, tm // dil, DIL_HD), lambda b, i: (b, 0, i, 0))] * 2
        tab_args += [cos, sin]
    out_shapes.append(jax.ShapeDtypeStruct((batch, seq, GLA_QK), F32))
    out_specs.append(pl.BlockSpec((None, tm, GLA_QK), lambda b, i: (b, i, 0)))
    return pl.pallas_call(
        _inproj_kernel,
        grid=(batch, seq // tm),
        in_specs=[pl.BlockSpec((None, tm, d), lambda b, i: (b, i, 0)),
                  _const_spec((1, d)),
                  pl.BlockSpec((None, W_COLS, d), lambda b, i: (layer, 0, 0),
                               pipeline_mode=pl.Buffered(1))] + tab_specs
                 + [_const_spec((ALOW_PAD, GLA_QK)), _const_spec((1, GLA_QK))],
        out_specs=out_specs,
        out_shape=out_shapes,
        scratch_shapes=[pltpu.VMEM((d // LANES, tm, LANES), F32)],
        compiler_params=_cparams(52),
        name="inproj",
    )(x3, g, w_all, *tab_args, wup, ba)


GLA_TS = 128
GLA_NCH = GLA_TS // GLA_CHUNK
GLA_FAST_SPAN = 60.0
GLA_FAST_TS = 256
GLA_FAST_UNROLL = 2
GLA_SEQ_PER_STEP = 2


def _band_col(h, r):
    base = (h // 2) * LANES
    if h % 2 == 0:
        return base + (LANES - r) % LANES
    return base + 2 * GLA_CHUNK - r


def _band_matrices():
    m = np.zeros((GLA_CHUNK, GLA_QK, 2 * LANES), np.float32)
    for r in range(GLA_CHUNK):
        for h in range(GLA_HEADS):
            m[r, h * GLA_DK:(h + 1) * GLA_DK, _band_col(h, r)] = 1.0
    return m


def _gla_kernel(q_ref, k_ref, v_ref, go_ref, la_ref, ng_ref, rmat_ref, y_ref, s_ref, st_ref, *, seq):
    ts, c, nch, nseq = GLA_TS, GLA_CHUNK, GLA_NCH, GLA_SEQ_PER_STEP
    s_ref[...] = jnp.zeros(s_ref.shape, F32)
    st_ref[...] = jnp.zeros(st_ref.shape, F32)

    def span_tile(ti, worst):
        r0 = pl.multiple_of(ti * ts, ts)
        for b in range(nseq):
            worst = jnp.maximum(worst, -jnp.sum(la_ref[b, pl.ds(r0, ts), :], axis=0, keepdims=True))
        return worst

    worst = lax.fori_loop(0, seq // ts, span_tile, jnp.zeros((1, GLA_QK), F32))
    decay_span = jnp.max(worst)

    row = lax.broadcasted_iota(jnp.int32, (ts, ts), 0)
    col = lax.broadcasted_iota(jnp.int32, (ts, ts), 1)
    same = (row >> 4) == (col >> 4)
    tri = jnp.where(same, jnp.where(col <= row, 1.0, 0.0), 0.0).astype(F32)
    blk = jnp.where(same, 1.0, 0.0).astype(F32)
    rowmod = lax.broadcasted_iota(jnp.int32, (ts, GLA_QK), 0) & (c - 1)
    keep_even = (col == 0) | (col > LANES - c)
    keep_odd = (col > c) & (col <= 2 * c)
    chunk_of_col = lax.broadcasted_iota(jnp.int32, (GLA_DK, ts), 1) >> 4
    q8_row = lax.broadcasted_iota(jnp.int32, (ts, nch * GLA_DK), 0) >> 4
    q8_col = lax.broadcasted_iota(jnp.int32, (ts, nch * GLA_DK), 1) >> 6

    def finish(o, b, r0, h, rows):
        on = o * lax.rsqrt(jnp.mean(o * o, axis=-1, keepdims=True) + EPS) * ng_ref[...]
        g = go_ref[b, pl.ds(r0, rows), h * GLA_DV:(h + 1) * GLA_DV].astype(F32)
        y_ref[b, pl.ds(r0, rows), h * GLA_DV:(h + 1) * GLA_DV] = (on * g * _sigmoid(g)).astype(BF16)

    def safe_tile(b, r0):
        q = q_ref[b, pl.ds(r0, ts), :].astype(F32) * (GLA_DK ** -0.5)
        k = k_ref[b, pl.ds(r0, ts), :].astype(F32)
        loga = la_ref[b, pl.ds(r0, ts), :]
        lc = jnp.dot(tri, loga, precision=lax.Precision.HIGHEST, preferred_element_type=F32)
        lend = jnp.dot(blk, loga, precision=lax.Precision.HIGHEST, preferred_element_type=F32)
        q_in = q * jnp.exp(lc)
        k_in = k * jnp.exp(lend - lc)

        band = _dot((q * k).astype(BF16), rmat_ref[0])
        rel = jnp.zeros_like(loga)
        for r in range(1, c):
            rel = rel + (loga if r == 1 else pltpu.roll(loga, r - 1, 0))
            term = jnp.where(rowmod >= r, q * pltpu.roll(k, r, 0) * jnp.exp(rel), 0.0)
            band = band + _dot(term.astype(BF16), rmat_ref[r])

        kin_t = k_in.T
        lend_t = lend.T
        for h in range(GLA_HEADS):
            vh = v_ref[b, pl.ds(r0, ts), h * GLA_DV:(h + 1) * GLA_DV]
            bt = band[:, (h // 2) * LANES:(h // 2 + 1) * LANES]
            if h % 2 == 0:
                p = pltpu.roll(jnp.where(keep_even, bt, 0.0), 0, 1, stride=1, stride_axis=0)
            else:
                p = pltpu.roll(jnp.where(keep_odd, bt, 0.0), LANES - 2 * c, 1, stride=1, stride_axis=0)
            o = _dot(p.astype(BF16), vh)

            kt = kin_t[h * GLA_DK:(h + 1) * GLA_DK, :]
            kst = jnp.concatenate(
                [jnp.where(chunk_of_col == n, kt, 0.0) for n in range(nch)], axis=0).astype(BF16)
            u = _dot(kst, vh)
            lt = lend_t[h * GLA_DK:(h + 1) * GLA_DK, :]
            st = s_ref[b * GLA_HEADS + h]
            states = []
            for n in range(nch):
                states.append(st)
                st = st * jnp.exp(lt[:, n * c:n * c + 1]) + u[n * GLA_DK:(n + 1) * GLA_DK, :]
            s_ref[b * GLA_HEADS + h] = st
            sst = jnp.concatenate(states, axis=0).astype(BF16)
            qh = q_in[:, h * GLA_DK:(h + 1) * GLA_DK]
            q8 = jnp.concatenate([qh] * nch, axis=1)
            qx = jnp.where(q8_row == q8_col, q8, 0.0).astype(BF16)
            finish(o + _dot(qx, sst), b, r0, h, ts)

    tf = GLA_FAST_TS
    causal = (lax.broadcasted_iota(jnp.int32, (tf, tf), 1) <= lax.broadcasted_iota(jnp.int32, (tf, tf), 0))
    tri_full = jnp.where(causal, 1.0, 0.0).astype(BF16)

    def fast_tile(b, r0):
        loga = la_ref[b, pl.ds(r0, tf), :]
        hi = loga.astype(BF16)
        rest = loga - hi.astype(F32)
        mid = rest.astype(BF16)
        lo = (rest - mid.astype(F32)).astype(BF16)
        cum = _dot(tri_full, hi) + _dot(tri_full, mid) + _dot(tri_full, lo)
        lend = cum[tf - 1:tf, :]
        dec = jnp.exp(lend)
        q = q_ref[b, pl.ds(r0, tf), :].astype(F32) * (GLA_DK ** -0.5)
        k = k_ref[b, pl.ds(r0, tf), :].astype(F32)
        qt = (q * jnp.exp(cum)).astype(BF16)
        kt = k * jnp.exp(-cum)
        ks = (kt * dec).astype(BF16)
        kt = kt.astype(BF16)
        for h in range(GLA_HEADS):
            ds = slice(h * GLA_DK, (h + 1) * GLA_DK)
            vh = v_ref[b, pl.ds(r0, tf), h * GLA_DV:(h + 1) * GLA_DV]
            p = jnp.where(causal, _dot_nt(qt[:, ds], kt[:, ds]), 0.0).astype(BF16)
            st = st_ref[b * GLA_HEADS + h]
            o = _dot(p, vh) + _dot_nt(qt[:, ds], st.astype(BF16))
            st_ref[b * GLA_HEADS + h] = st * dec[:, ds] + _dot_tn(vh, ks[:, ds])
            finish(o, b, r0, h, tf)

    def run(tile_fn, rows, unroll):
        def body(ti, carry):
            r0 = pl.multiple_of(ti * rows, rows)
            for b in range(nseq):
                tile_fn(b, r0)
            return carry
        lax.fori_loop(0, seq // rows, body, 0, unroll=unroll)

    fast_ok = decay_span * (tf // ts) <= GLA_FAST_SPAN

    @pl.when(fast_ok)
    def _():
        run(fast_tile, tf, GLA_FAST_UNROLL)

    @pl.when(jnp.logical_not(fast_ok))
    def _():
        run(safe_tile, ts, 1)


def _gla(zn, loga, ng, rmat):
    batch, seq, _ = zn.shape
    nseq = GLA_SEQ_PER_STEP
    assert batch % nseq == 0

    def zspec(width, blk_idx):
        return pl.BlockSpec((nseq, seq, width), lambda b: (b, 0, blk_idx))

    return pl.pallas_call(
        functools.partial(_gla_kernel, seq=seq),
        grid=(batch // nseq,),
        in_specs=[zspec(GLA_QK, ZB_GQ), zspec(GLA_QK, ZB_GK), zspec(GLA_V, ZB_GV),
                  zspec(GLA_V, ZB_GO), zspec(GLA_QK, 0),
                  _const_spec((1, GLA_DV)),
                  _const_spec((GLA_CHUNK, GLA_QK, 2 * LANES))],
        out_specs=pl.BlockSpec((nseq, seq, GLA_V), lambda b: (b, 0, 0)),
        out_shape=jax.ShapeDtypeStruct((batch, seq, GLA_V), BF16),
        scratch_shapes=[pltpu.VMEM((nseq * GLA_HEADS, GLA_DK, GLA_DV), F32),
                        pltpu.VMEM((nseq * GLA_HEADS, GLA_DV, GLA_DK), F32)],
        compiler_params=_cparams(48),
        name="gla",
    )(zn, zn, zn, zn, loga, ng, rmat)


DIL_UNROLL = 16
DIL_ROWS_PER_STEP = 4096


def _dil_kernel(q_ref, k_ref, v_ref, o_ref, st_ref, *, n_res, sub_len, chained):
    blk = DIL_BLOCK
    nb = sub_len // blk
    nk = 2 * blk
    qi = lax.broadcasted_iota(jnp.int32, (blk, nk), 0)
    km = lax.broadcasted_iota(jnp.int32, (blk, nk), 1)
    bias_window = jnp.where((km >= qi) & (km <= qi + blk), 0.0, NEG_BIG).astype(F32)
    bias_lead = jnp.where(km <= qi, 0.0, NEG_BIG).astype(F32)
    bias_trail = jnp.where((km >= blk) & (km <= qi + blk), 0.0, NEG_BIG).astype(F32)
    ones = jnp.ones((nk, DIL_HD), BF16)
    lane = lax.broadcasted_iota(jnp.int32, (blk, LANES), 1)
    lanes_per_head = LANES // DIL_HEADS
    max_lane = (lane & (lanes_per_head // 2)) == 0

    def unit(res, q0, k0, bias):
        stats = jnp.zeros((blk, LANES), F32)
        for h in range(DIL_HEADS):
            sl = slice(h * DIL_HD, (h + 1) * DIL_HD)
            q = q_ref[res, pl.ds(q0, blk), sl]
            kc = k_ref[res, pl.ds(k0, nk), sl]
            vc = jnp.concatenate([v_ref[res, pl.ds(k0, nk), sl], ones], axis=1)
            s = _dot_nt(q, kc) + bias
            mx = jnp.max(s, axis=-1, keepdims=True)
            p = jnp.exp(s - mx)
            ov = _dot(p.astype(BF16), vc)
            o_ref[res, pl.ds(q0, blk), sl] = ov[:, :DIL_HD].astype(BF16)
            stats = jnp.where(lane >= h * lanes_per_head, jnp.where(max_lane, mx, ov[:, DIL_HD:]), stats)
        st_ref[res, pl.ds(q0, blk), :] = stats

    assert (n_res * nb) % DIL_UNROLL == 0

    def body(g, carry):
        for u in range(DIL_UNROLL):
            uid = g * DIL_UNROLL + u
            res = uid >> (nb.bit_length() - 1)
            i = uid & (nb - 1)
            q0 = pl.multiple_of(i * blk, blk)
            if chained:
                k0 = pl.multiple_of(jnp.maximum(q0 - blk, 0), blk)
                bias = jnp.where(i == 0, bias_lead, bias_window)
            else:
                k0 = pl.multiple_of(jnp.minimum(q0, sub_len - nk), blk)
                bias = jnp.where(i == nb - 1, bias_trail, bias_lead)
            unit(res, q0, k0, bias)
        return carry

    lax.fori_loop(0, (n_res * nb) // DIL_UNROLL, body, 0)


def _dil_group(zq, chained, name):
    _, nsub, sub_len, _ = zq.shape
    n_res = DIL_ROWS_PER_STEP // sub_len
    assert sub_len >= 2 * DIL_BLOCK and nsub % n_res == 0

    def zspec(part):
        return pl.BlockSpec((None, n_res, sub_len, DIL_GW), lambda s: (part, s, 0, 0))

    return pl.pallas_call(
        functools.partial(_dil_kernel, n_res=n_res, sub_len=sub_len, chained=chained),
        grid=(nsub // n_res,),
        in_specs=[zspec(0), zspec(1), zspec(2)],
        out_specs=[pl.BlockSpec((n_res, sub_len, DIL_GW), lambda s: (s, 0, 0)),
                   pl.BlockSpec((n_res, sub_len, LANES), lambda s: (s, 0, 0))],
        out_shape=[jax.ShapeDtypeStruct((nsub, sub_len, DIL_GW), BF16),
                   jax.ShapeDtypeStruct((nsub, sub_len, LANES), F32)],
        compiler_params=_cparams(48),
        name=name,
    )(zq, zq, zq)


MERGE_TM = 512


def _shifted(prev_rows, cur, tm):
    full = jnp.concatenate([prev_rows, cur], axis=0)
    d1 = pltpu.roll(full, 1, 0)[SUBLANES:SUBLANES + tm]
    d2 = pltpu.roll(full, 2, 0)[SUBLANES:SUBLANES + tm]
    return d1, d2


def _merge_kernel(x_ref, scb_ref, scc_ref, scx_ref, ygla_ref, o0_ref, o1_ref, o2_ref,
                  l0_ref, l1_ref, l2_ref, gpre_ref, wg_ref, bg_ref, wbr_ref, wmix_ref,
                  gpost_ref, cw_ref, out_ref, halo_ref, oi1_ref, oi2_ref, li1_ref, li2_ref):
    tm = MERGE_TM

    @pl.when(pl.program_id(1) == 0)
    def _():
        halo_ref[...] = jnp.zeros(halo_ref.shape, F32)

    x = x_ref[...]
    h = _rms(x, gpre_ref[...]).astype(BF16)

    u = scc_ref[...].astype(F32) * scx_ref[...].astype(F32)
    u1, u2 = _shifted(halo_ref[...], u, tm)
    halo_ref[...] = u[tm - SUBLANES:, :]
    cw = cw_ref[...]
    y_sc = scb_ref[...].astype(F32) * (cw[0:1, :] * u2 + cw[1:2, :] * u1 + cw[2:3, :] * u)

    for (o_ref, l_ref, oi_ref, li_ref), (_, dil) in zip(
            ((o1_ref, l1_ref, oi1_ref, li1_ref), (o2_ref, l2_ref, oi2_ref, li2_ref)), DIL_PATTERNS[1:]):
        n = tm // dil
        for r in range(dil):
            o_r = o_ref[r].astype(F32)
            for hh in range(DIL_HEADS):
                oi_ref[hh, pl.ds(r, n, stride=dil), :] = o_r[:, hh * DIL_HD:(hh + 1) * DIL_HD]
            li_ref[pl.ds(r, n, stride=dil), :] = l_ref[r]

    stats = (l0_ref[0], li1_ref[...], li2_ref[...])
    lanes_per_head = LANES // DIL_HEADS
    parts = []
    for hh in range(DIL_HEADS):
        sl = slice(hh * DIL_HD, (hh + 1) * DIL_HD)
        lm = hh * lanes_per_head
        ld = lm + lanes_per_head // 2
        ma, mb, mc = (v[:, lm:lm + 1] for v in stats)
        da, db, dc = (v[:, ld:ld + 1] for v in stats)
        m = jnp.maximum(jnp.maximum(ma, mb), mc)
        ea, eb, ec = jnp.exp(ma - m), jnp.exp(mb - m), jnp.exp(mc - m)
        inv = 1.0 / (ea * da + eb * db + ec * dc)
        parts.append((ea * inv) * o0_ref[0, :, sl].astype(F32)
                     + (eb * inv) * oi1_ref[hh] + (ec * inv) * oi2_ref[hh])
    y_dil = jnp.concatenate(parts, axis=1)

    merged = jnp.zeros((tm, D_MODEL), F32)
    branches = (ygla_ref[...], y_sc.astype(BF16), y_dil.astype(BF16))
    for g, br in enumerate(branches):
        cs = slice(g * D_MODEL, (g + 1) * D_MODEL)
        gate = _sigmoid(_dot(h, wg_ref[:, cs]) + bg_ref[:, cs])
        merged = merged + gate * _dot(br, wbr_ref[g])
    mix = _dot(merged.astype(BF16), wmix_ref[...])
    out_ref[...] = x + _rms(mix, gpost_ref[...])


def _merge(x3, zn, ygla, o_list, l_list, gpre, wg, bg, wbr, wmix, gpost, cw):
    batch, seq, d = x3.shape
    tm = MERGE_TM

    def rows(width, blk_idx=0):
        return pl.BlockSpec((None, tm, width), lambda b, i: (b, i, blk_idx))

    def grouped(dil, width):
        return pl.BlockSpec((None, dil, tm // dil, width), lambda b, i: (b, 0, i, 0))

    dils = [dil for _, dil in DIL_PATTERNS]
    return pl.pallas_call(
        _merge_kernel,
        grid=(batch, seq // tm),
        in_specs=[rows(d),
                  rows(BRANCH_WIDTH, ZB_SCB), rows(BRANCH_WIDTH, ZB_SCC), rows(BRANCH_WIDTH, ZB_SCX),
                  rows(GLA_V)]
                 + [grouped(dil, DIL_GW) for dil in dils]
                 + [grouped(dil, LANES) for dil in dils]
                 + [_const_spec((1, d)),
                    _const_spec((d, 3 * d)),
                    _const_spec((1, 3 * d)),
                    _const_spec((3, BRANCH_WIDTH, d)),
                    _const_spec((d, d)),
                    _const_spec((1, d)),
                    _const_spec((CONV_K, BRANCH_WIDTH))],
        out_specs=rows(d),
        out_shape=jax.ShapeDtypeStruct((batch, seq, d), F32),
        scratch_shapes=[pltpu.VMEM((SUBLANES, BRANCH_WIDTH), F32),
                        pltpu.VMEM((DIL_HEADS, tm, DIL_HD), F32), pltpu.VMEM((DIL_HEADS, tm, DIL_HD), F32),
                        pltpu.VMEM((tm, LANES), F32), pltpu.VMEM((tm, LANES), F32)],
        compiler_params=_cparams(48, ("arbitrary", "arbitrary")),
        name="merge",
    )(x3, zn, zn, zn, ygla, *o_list, *l_list, gpre, wg, bg, wbr, wmix, gpost, cw)


FFN_TM = 512
FFN_CHUNKS = ((0, 1536), (1536, D_FF))


def _ffn_kernel(x_ref, g1_ref, wg_ref, wu_ref, cw_ref, cb_ref, wd_ref, g2_ref, out_ref, halo_ref):
    tm = FFN_TM

    @pl.when(pl.program_id(1) == 0)
    def _():
        halo_ref[...] = jnp.zeros(halo_ref.shape, F32)

    x = x_ref[...]
    h = _rms(x, g1_ref[...]).astype(BF16)
    acc = jnp.zeros((tm, D_MODEL), F32)
    for c0, c1 in FFN_CHUNKS:
        a = _dot(h, wg_ref[:, c0:c1])
        a1, a2 = _shifted(halo_ref[:, c0:c1], a, tm)
        halo_ref[:, c0:c1] = a[tm - SUBLANES:, :]
        cw = cw_ref[:, c0:c1]
        gt = cw[0:1, :] * a2 + cw[1:2, :] * a1 + cw[2:3, :] * a + cb_ref[:, c0:c1]
        ge = 0.5 * gt * (1.0 + jnp.tanh(0.7978845608028654 * (gt + 0.044715 * (gt * gt * gt))))
        up = _dot(h, wu_ref[:, c0:c1])
        acc = acc + _dot((ge * up).astype(BF16), wd_ref[c0:c1, :])
    out_ref[...] = x + _rms(acc, g2_ref[...])


def _ffn(x3, g1, wg, wu, cw, cb, wd, g2):
    batch, seq, d = x3.shape
    tm = FFN_TM
    xspec = pl.BlockSpec((None, tm, d), lambda b, i: (b, i, 0))
    return pl.pallas_call(
        _ffn_kernel,
        grid=(batch, seq // tm),
        in_specs=[xspec,
                  _const_spec((1, d)),
                  _const_spec((d, D_FF)),
                  _const_spec((d, D_FF)),
                  _const_spec((CONV_K, D_FF)),
                  _const_spec((1, D_FF)),
                  _const_spec((D_FF, d)),
                  _const_spec((1, d))],
        out_specs=xspec,
        out_shape=jax.ShapeDtypeStruct((batch, seq, d), F32),
        scratch_shapes=[pltpu.VMEM((SUBLANES, D_FF), F32)],
        compiler_params=_cparams(52, ("arbitrary", "arbitrary")),
        name="ffn",
    )(x3, g1, wg, wu, cw, cb, wd, g2)


def kernel(x, positions, w_in, w_alpha_up, b_alpha, gla_norm_g, sc_conv_w, w_gate, b_gate, w_branch,
           w_mix_out, pre_mix_g, post_mix_g, pre_ffn_g, post_ffn_g, w_ff_gate, w_ff_up, ff_conv_w,
           ff_conv_b, w_ff_down):
    batch, seq, d = x.shape
    depth = w_in.shape[0]
    tables = _rope_tables(positions)
    w_perm = _permute_w_in(w_in)
    rmat = jnp.asarray(_band_matrices(), BF16)

    for l in range(depth):
        wup = jnp.concatenate(
            [w_alpha_up[l], jnp.zeros((ALOW_PAD - GLA_RANK, GLA_QK), F32)], axis=0).astype(BF16)
        zn, z0, z1, z2, loga = _inproj(x, pre_mix_g[l].reshape(1, d), w_perm, l, tables,
                                       wup, b_alpha[l].reshape(1, GLA_QK))
        ygla = _gla(zn, loga, gla_norm_g[l].reshape(1, GLA_DV), rmat)
        o_list, l_list = [], []
        for zq, (_, dil) in zip((z0, z1, z2), DIL_PATTERNS):
            sub_len = seq // dil
            if sub_len == DIL_BLOCK:
                o, st = _dil_group(zq.reshape(3, batch, seq, DIL_GW), False, f"dil_attn_d{dil}")
            else:
                o, st = _dil_group(zq.reshape(3, batch * dil, sub_len, DIL_GW), True, f"dil_attn_d{dil}")
            o_list.append(o.reshape(batch, dil, sub_len, DIL_GW))
            l_list.append(st.reshape(batch, dil, sub_len, LANES))
        x = _merge(x, zn, ygla, o_list, l_list, pre_mix_g[l].reshape(1, d),
                   w_gate[l].astype(BF16), b_gate[l].reshape(1, 3 * d), w_branch[l].astype(BF16),
                   w_mix_out[l].astype(BF16), post_mix_g[l].reshape(1, d), sc_conv_w[l])
        x = _ffn(x, pre_ffn_g[l].reshape(1, d), w_ff_gate[l].astype(BF16), w_ff_up[l].astype(BF16),
                 ff_conv_w[l], ff_conv_b[l].reshape(1, D_FF), w_ff_down[l].astype(BF16),
                 post_ffn_g[l].reshape(1, d))
    return x
```

```python
import functools

import jax
import jax.numpy as jnp
import numpy as np
from jax import lax
from jax.experimental import pallas as pl
from jax.experimental.pallas import tpu as pltpu

F32 = jnp.float32
BF16 = jnp.bfloat16

D_MODEL = 1024
EPS = 1e-6
BRANCH_WIDTH = 512
GLA_HEADS = 4
GLA_DV = 128
GLA_DK = 64
GLA_RANK = 16
GLA_TAU = 16.0
GLA_CHUNK = 16
GLA_QK = GLA_HEADS * GLA_DK
GLA_V = GLA_HEADS * GLA_DV
CONV_K = 3
DIL_PATTERNS = ((128, 1), (512, 4), (2048, 16))
DIL_GROUPS = len(DIL_PATTERNS)
DIL_HEADS = 4
DIL_HD = 128
DIL_GW = DIL_HEADS * DIL_HD
DIL_BLOCK = 128
ROPE_THETA = 10000.0
D_FF = 2816

LANES = 128
SUBLANES = 8
MIB = 1024 * 1024

ALOW_PAD = LANES
ZN_COLS = 2 * GLA_V + 3 * BRANCH_WIDTH + 2 * GLA_QK
ZB_GV, ZB_GO, ZB_SCB, ZB_SCC, ZB_SCX = 0, 1, 2, 3, 4
ZB_GQ, ZB_GK = 10, 11
ZG_COLS = 3 * DIL_GW
W_ALOW = DIL_GROUPS * ZG_COLS + ZN_COLS
W_COLS = W_ALOW + ALOW_PAD

NEG_BIG = -1e30


def _cparams(vmem_mib, sem=None):
    return pltpu.CompilerParams(dimension_semantics=sem, vmem_limit_bytes=int(vmem_mib * MIB))


def _const_spec(shape):
    nd = len(shape)
    return pl.BlockSpec(shape, lambda *_: (0,) * nd, pipeline_mode=pl.Buffered(1))


def _layer_spec(shape, layer):
    nd = len(shape)
    return pl.BlockSpec((None,) + tuple(shape), lambda *_: (layer,) + (0,) * nd, pipeline_mode=pl.Buffered(1))


def _rms(xf, g):
    r = lax.rsqrt(jnp.mean(xf * xf, axis=-1, keepdims=True) + EPS)
    return xf * r * g


def _sigmoid(v):
    return 1.0 / (1.0 + jnp.exp(-v))


def _dot(a, b):
    return jnp.dot(a, b, preferred_element_type=F32)


def _dot_nt(a, b):
    return lax.dot_general(a, b, (((1,), (1,)), ((), ())), preferred_element_type=F32)


def _dot_tn(a, b):
    return lax.dot_general(a, b, (((0,), (0,)), ((), ())), preferred_element_type=F32)


def _rope_kernel(pos_ref, inv_ref, sgn_ref, c0_ref, s0_ref, c1_ref, s1_ref, c2_ref, s2_ref, *, seq):
    half = seq // 2
    lane = lax.broadcasted_iota(jnp.int32, (half, DIL_HD), 1)
    low = lane < DIL_HD // 2
    pos = jnp.where(low, pos_ref[0:half, :], pos_ref[half:seq, :]).astype(F32)
    ang = pos * inv_ref[...]
    cs, sn = jnp.cos(ang), jnp.sin(ang)
    cs_sw, sn_sw = pltpu.roll(cs, DIL_HD // 2, 1), pltpu.roll(sn, DIL_HD // 2, 1)
    c0_ref[0, 0:half, :] = jnp.where(low, cs, cs_sw)
    c0_ref[0, half:seq, :] = jnp.where(low, cs_sw, cs)
    s0_ref[0, 0:half, :] = jnp.where(low, sn, sn_sw) * sgn_ref[...]
    s0_ref[0, half:seq, :] = jnp.where(low, sn_sw, sn) * sgn_ref[...]
    for (c_ref, s_ref), (_, dil) in zip(((c1_ref, s1_ref), (c2_ref, s2_ref)), DIL_PATTERNS[1:]):
        n = seq // dil
        for r in range(dil):
            c_ref[r] = c0_ref[0, pl.ds(r, n, stride=dil), :]
            s_ref[r] = s0_ref[0, pl.ds(r, n, stride=dil), :]


def _rope_tables(positions):
    batch, seq = positions.shape
    inv = ROPE_THETA ** (-jnp.arange(0, DIL_HD, 2, dtype=F32) / DIL_HD)
    inv2 = jnp.concatenate([inv, inv]).reshape(1, DIL_HD)
    half = DIL_HD // 2
    sgn = jnp.concatenate([-jnp.ones((half,), F32), jnp.ones((half,), F32)]).reshape(1, DIL_HD)
    shapes, specs = [], []
    for _, dil in DIL_PATTERNS:
        shp = (batch, dil, seq // dil, DIL_HD)
        for _ in range(2):
            shapes.append(jax.ShapeDtypeStruct(shp, F32))
            specs.append(pl.BlockSpec((None,) + shp[1:], lambda b: (b, 0, 0, 0)))
    outs = pl.pallas_call(
        functools.partial(_rope_kernel, seq=seq),
        grid=(batch,),
        in_specs=[pl.BlockSpec((None, seq, 1), lambda b: (b, 0, 0)),
                  pl.BlockSpec((1, DIL_HD), lambda b: (0, 0)),
                  pl.BlockSpec((1, DIL_HD), lambda b: (0, 0))],
        out_specs=specs,
        out_shape=shapes,
        compiler_params=_cparams(32),
        name="rope_tables",
    )(positions.reshape(batch, seq, 1), inv2, sgn)
    return [(outs[2 * g], outs[2 * g + 1]) for g in range(DIL_GROUPS)]


CAST_ROWS = 512


def _cast_kernel(w_ref, o_ref):
    o_ref[...] = w_ref[...].astype(BF16)


def _to_bf16(w):
    rows, cols = w.size // w.shape[-1], w.shape[-1]
    out = pl.pallas_call(
        _cast_kernel,
        grid=(rows // CAST_ROWS,),
        in_specs=[pl.BlockSpec((CAST_ROWS, cols), lambda i: (i, 0))],
        out_specs=pl.BlockSpec((CAST_ROWS, cols), lambda i: (i, 0)),
        out_shape=jax.ShapeDtypeStruct((rows, cols), BF16),
        compiler_params=_cparams(32),
        name="weights_to_bf16",
    )(w.reshape(rows, cols))
    return out.reshape(w.shape)


_W_IN_SEGMENTS = tuple((o + gi * DIL_GW, DIL_GW) for gi in range(DIL_GROUPS) for o in (3088, 4624, 6160)) + (
    (512, 2 * GLA_V), (1552, 3 * BRANCH_WIDTH), (0, 2 * GLA_QK), (1536, GLA_RANK))
W_PREP_LANES = 128


def _w_in_kernel(w_ref, o_ref):
    dst = 0
    for src, width in _W_IN_SEGMENTS:
        o_ref[dst:dst + width, :] = w_ref[src:src + width, :].astype(BF16)
        dst += width
    o_ref[dst:W_COLS, :] = jnp.zeros((W_COLS - dst, o_ref.shape[1]), BF16)


def _permute_w_in(w_in):
    depth, d, cols = w_in.shape
    return pl.pallas_call(
        _w_in_kernel,
        grid=(depth, d // W_PREP_LANES),
        in_specs=[pl.BlockSpec((None, cols, W_PREP_LANES), lambda l, i: (l, 0, i))],
        out_specs=pl.BlockSpec((None, W_COLS, W_PREP_LANES), lambda l, i: (l, 0, i)),
        out_shape=jax.ShapeDtypeStruct((depth, W_COLS, d), BF16),
        compiler_params=_cparams(32),
        name="w_in_permute",
    )(jnp.swapaxes(w_in, 1, 2))


INPROJ_TM = 512
INPROJ_CH = 1024


def _inproj_kernel(x_ref, g_ref, w_ref, c0_ref, s0_ref, c1_ref, s1_ref, c2_ref, s2_ref, wup_ref, ba_ref,
                   zn_ref, z0_ref, z1_ref, z2_ref, la_ref, h_ref):
    tm = INPROJ_TM
    nlt = D_MODEL // LANES
    hf = _rms(x_ref[...], g_ref[...])
    for j in range(nlt):
        h_ref[j] = hf[:, j * LANES:(j + 1) * LANES]
    h = hf.astype(BF16)

    groups = ((z0_ref, c0_ref, s0_ref), (z1_ref, c1_ref, s1_ref), (z2_ref, c2_ref, s2_ref))
    for gi, (z_ref, c_ref, s_ref) in enumerate(groups):
        dil = DIL_PATTERNS[gi][1]
        n = tm // dil
        if dil == 1:
            hp = h
        else:
            hp = jnp.concatenate(
                [jnp.concatenate([h_ref[j, pl.ds(r, n, stride=dil), :] for j in range(nlt)], axis=1)
                 for r in range(dil)], axis=0).astype(BF16)
        cs = jnp.concatenate([c_ref[r] for r in range(dil)], axis=0)
        sn = jnp.concatenate([s_ref[r] for r in range(dil)], axis=0)
        for part in range(3):
            w0 = gi * ZG_COLS + part * DIL_GW
            res = _dot_nt(hp, w_ref[w0:w0 + DIL_GW, :])
            if part < 2:
                sc = DIL_HD ** -0.5 if part == 0 else 1.0
                heads = []
                for hh in range(DIL_HEADS):
                    xh = res[:, hh * DIL_HD:(hh + 1) * DIL_HD]
                    heads.append(xh * (cs * sc) + pltpu.roll(xh, DIL_HD // 2, 1) * (sn * sc))
                res = jnp.concatenate(heads, axis=1)
            resb = res.astype(BF16)
            for r in range(dil):
                z_ref[part, r] = resb[r * n:(r + 1) * n, :]

    wn = DIL_GROUPS * ZG_COLS
    for c0 in range(0, ZN_COLS, INPROJ_CH):
        c1 = min(c0 + INPROJ_CH, ZN_COLS)
        zn_ref[:, c0:c1] = _dot_nt(h, w_ref[wn + c0:wn + c1, :]).astype(BF16)

    a_low = _dot_nt(h, w_ref[W_ALOW:W_COLS, :]).astype(BF16)
    xa = _dot(a_low, wup_ref[...]) + ba_ref[...]
    la_ref[...] = (jnp.minimum(xa, 0.0) - jnp.log(1.0 + jnp.exp(-jnp.abs(xa)))) * (1.0 / GLA_TAU)


def _inproj(x3, g, w_all, layer, tables, wup, ba):
    batch, seq, d = x3.shape
    tm = INPROJ_TM
    out_shapes = [jax.ShapeDtypeStruct((batch, seq, ZN_COLS), BF16)]
    out_specs = [pl.BlockSpec((None, tm, ZN_COLS), lambda b, i: (b, i, 0))]
    tab_specs, tab_args = [], []
    for (_, dil), (cos, sin) in zip(DIL_PATTERNS, tables):
        out_shapes.append(jax.ShapeDtypeStruct((3, batch, dil, seq // dil, DIL_GW), BF16))
        out_specs.append(pl.BlockSpec((3, None, dil, tm // dil, DIL_GW), lambda b, i: (0, b, 0, i, 0)))
        tab_specs += [pl.BlockSpec((None, dil, tm // dil, DIL_HD), lambda b, i: (b, 0, i, 0))] * 2
        tab_args += [cos, sin]
    out_shapes.append(jax.ShapeDtypeStruct((batch, seq, GLA_QK), F32))
    out_specs.append(pl.BlockSpec((None, tm, GLA_QK), lambda b, i: (b, i, 0)))
    return pl.pallas_call(
        _inproj_kernel,
        grid=(batch, seq // tm),
        in_specs=[pl.BlockSpec((None, tm, d), lambda b, i: (b, i, 0)),
                  _const_spec((1, d)),
                  pl.BlockSpec((None, W_COLS, d), lambda b, i: (layer, 0, 0),
                               pipeline_mode=pl.Buffered(1))] + tab_specs
                 + [_const_spec((ALOW_PAD, GLA_QK)), _const_spec((1, GLA_QK))],
        out_specs=out_specs,
        out_shape=out_shapes,
        scratch_shapes=[pltpu.VMEM((d // LANES, tm, LANES), F32)],
        compiler_params=_cparams(52),
        name="inproj",
    )(x3, g, w_all, *tab_args, wup, ba)


GLA_TS = 128
GLA_NCH = GLA_TS // GLA_CHUNK
GLA_FAST_SPAN = 60.0
GLA_FAST_TS = 256
GLA_FAST_UNROLL = 2
GLA_SEQ_PER_STEP = 2


def _band_col(h, r):
    base = (h // 2) * LANES
    if h % 2 == 0:
        return base + (LANES - r) % LANES
    return base + 2 * GLA_CHUNK - r


def _band_matrices():
    m = np.zeros((GLA_CHUNK, GLA_QK, 2 * LANES), np.float32)
    for r in range(GLA_CHUNK):
        for h in range(GLA_HEADS):
            m[r, h * GLA_DK:(h + 1) * GLA_DK, _band_col(h, r)] = 1.0
    return m


def _gla_kernel(q_ref, k_ref, v_ref, go_ref, la_ref, ng_ref, rmat_ref, y_ref, s_ref, st_ref, *, seq):
    ts, c, nch, nseq = GLA_TS, GLA_CHUNK, GLA_NCH, GLA_SEQ_PER_STEP
    s_ref[...] = jnp.zeros(s_ref.shape, F32)
    st_ref[...] = jnp.zeros(st_ref.shape, F32)

    def span_tile(ti, worst):
        r0 = pl.multiple_of(ti * ts, ts)
        for b in range(nseq):
            worst = jnp.maximum(worst, -jnp.sum(la_ref[b, pl.ds(r0, ts), :], axis=0, keepdims=True))
        return worst

    worst = lax.fori_loop(0, seq // ts, span_tile, jnp.zeros((1, GLA_QK), F32))
    decay_span = jnp.max(worst)

    row = lax.broadcasted_iota(jnp.int32, (ts, ts), 0)
    col = lax.broadcasted_iota(jnp.int32, (ts, ts), 1)
    same = (row >> 4) == (col >> 4)
    tri = jnp.where(same, jnp.where(col <= row, 1.0, 0.0), 0.0).astype(F32)
    blk = jnp.where(same, 1.0, 0.0).astype(F32)
    rowmod = lax.broadcasted_iota(jnp.int32, (ts, GLA_QK), 0) & (c - 1)
    keep_even = (col == 0) | (col > LANES - c)
    keep_odd = (col > c) & (col <= 2 * c)
    chunk_of_col = lax.broadcasted_iota(jnp.int32, (GLA_DK, ts), 1) >> 4
    q8_row = lax.broadcasted_iota(jnp.int32, (ts, nch * GLA_DK), 0) >> 4
    q8_col = lax.broadcasted_iota(jnp.int32, (ts, nch * GLA_DK), 1) >> 6

    def finish(o, b, r0, h, rows):
        on = o * lax.rsqrt(jnp.mean(o * o, axis=-1, keepdims=True) + EPS) * ng_ref[...]
        g = go_ref[b, pl.ds(r0, rows), h * GLA_DV:(h + 1) * GLA_DV].astype(F32)
        y_ref[b, pl.ds(r0, rows), h * GLA_DV:(h + 1) * GLA_DV] = (on * g * _sigmoid(g)).astype(BF16)

    def safe_tile(b, r0):
        q = q_ref[b, pl.ds(r0, ts), :].astype(F32) * (GLA_DK ** -0.5)
        k = k_ref[b, pl.ds(r0, ts), :].astype(F32)
        loga = la_ref[b, pl.ds(r0, ts), :]
        lc = jnp.dot(tri, loga, precision=lax.Precision.HIGHEST, preferred_element_type=F32)
        lend = jnp.dot(blk, loga, precision=lax.Precision.HIGHEST, preferred_element_type=F32)
        q_in = q * jnp.exp(lc)
        k_in = k * jnp.exp(lend - lc)

        band = _dot((q * k).astype(BF16), rmat_ref[0])
        rel = jnp.zeros_like(loga)
        for r in range(1, c):
            rel = rel + (loga if r == 1 else pltpu.roll(loga, r - 1, 0))
            term = jnp.where(rowmod >= r, q * pltpu.roll(k, r, 0) * jnp.exp(rel), 0.0)
            band = band + _dot(term.astype(BF16), rmat_ref[r])

        kin_t = k_in.T
        lend_t = lend.T
        for h in range(GLA_HEADS):
            vh = v_ref[b, pl.ds(r0, ts), h * GLA_DV:(h + 1) * GLA_DV]
            bt = band[:, (h // 2) * LANES:(h // 2 + 1) * LANES]
            if h % 2 == 0:
                p = pltpu.roll(jnp.where(keep_even, bt, 0.0), 0, 1, stride=1, stride_axis=0)
            else:
                p = pltpu.roll(jnp.where(keep_odd, bt, 0.0), LANES - 2 * c, 1, stride=1, stride_axis=0)
            o = _dot(p.astype(BF16), vh)

            kt = kin_t[h * GLA_DK:(h + 1) * GLA_DK, :]
            kst = jnp.concatenate(
                [jnp.where(chunk_of_col == n, kt, 0.0) for n in range(nch)], axis=0).astype(BF16)
            u = _dot(kst, vh)
            lt = lend_t[h * GLA_DK:(h + 1) * GLA_DK, :]
            st = s_ref[b * GLA_HEADS + h]
            states = []
            for n in range(nch):
                states.append(st)
                st = st * jnp.exp(lt[:, n * c:n * c + 1]) + u[n * GLA_DK:(n + 1) * GLA_DK, :]
            s_ref[b * GLA_HEADS + h] = st
            sst = jnp.concatenate(states, axis=0).astype(BF16)
            qh = q_in[:, h * GLA_DK:(h + 1) * GLA_DK]
            q8 = jnp.concatenate([qh] * nch, axis=1)
            qx = jnp.where(q8_row == q8_col, q8, 0.0).astype(BF16)
            finish(o + _dot(qx, sst), b, r0, h, ts)

    tf = GLA_FAST_TS
    causal = (lax.broadcasted_iota(jnp.int32, (tf, tf), 1) <= lax.broadcasted_iota(jnp.int32, (tf, tf), 0))
    tri_full = jnp.where(causal, 1.0, 0.0).astype(BF16)

    def fast_tile(b, r0):
        loga = la_ref[b, pl.ds(r0, tf), :]
        hi = loga.astype(BF16)
        rest = loga - hi.astype(F32)
        mid = rest.astype(BF16)
        lo = (rest - mid.astype(F32)).astype(BF16)
        cum = _dot(tri_full, hi) + _dot(tri_full, mid) + _dot(tri_full, lo)
        lend = cum[tf - 1:tf, :]
        dec = jnp.exp(lend)
        q = q_ref[b, pl.ds(r0, tf), :].astype(F32) * (GLA_DK ** -0.5)
        k = k_ref[b, pl.ds(r0, tf), :].astype(F32)
        qt = (q * jnp.exp(cum)).astype(BF16)
        kt = k * jnp.exp(-cum)
        ks = (kt * dec).astype(BF16)
        kt = kt.astype(BF16)
        for h in range(GLA_HEADS):
            ds = slice(h * GLA_DK, (h + 1) * GLA_DK)
            vh = v_ref[b, pl.ds(r0, tf), h * GLA_DV:(h + 1) * GLA_DV]
            p = jnp.where(causal, _dot_nt(qt[:, ds], kt[:, ds]), 0.0).astype(BF16)
            st = st_ref[b * GLA_HEADS + h]
            o = _dot(p, vh) + _dot_nt(qt[:, ds], st.astype(BF16))
            st_ref[b * GLA_HEADS + h] = st * dec[:, ds] + _dot_tn(vh, ks[:, ds])
            finish(o, b, r0, h, tf)

    def run(tile_fn, rows, unroll):
        def body(ti, carry):
            r0 = pl.multiple_of(ti * rows, rows)
            for b in range(nseq):
                tile_fn(b, r0)
            return carry
        lax.fori_loop(0, seq // rows, body, 0, unroll=unroll)

    fast_ok = decay_span * (tf // ts) <= GLA_FAST_SPAN

    @pl.when(fast_ok)
    def _():
        run(fast_tile, tf, GLA_FAST_UNROLL)

    @pl.when(jnp.logical_not(fast_ok))
    def _():
        run(safe_tile, ts, 1)


def _gla(zn, loga, ng, rmat):
    batch, seq, _ = zn.shape
    nseq = GLA_SEQ_PER_STEP
    assert batch % nseq == 0

    def zspec(width, blk_idx):
        return pl.BlockSpec((nseq, seq, width), lambda b: (b, 0, blk_idx))

    return pl.pallas_call(
        functools.partial(_gla_kernel, seq=seq),
        grid=(batch // nseq,),
        in_specs=[zspec(GLA_QK, ZB_GQ), zspec(GLA_QK, ZB_GK), zspec(GLA_V, ZB_GV),
                  zspec(GLA_V, ZB_GO), zspec(GLA_QK, 0),
                  _const_spec((1, GLA_DV)),
                  _const_spec((GLA_CHUNK, GLA_QK, 2 * LANES))],
        out_specs=pl.BlockSpec((nseq, seq, GLA_V), lambda b: (b, 0, 0)),
        out_shape=jax.ShapeDtypeStruct((batch, seq, GLA_V), BF16),
        scratch_shapes=[pltpu.VMEM((nseq * GLA_HEADS, GLA_DK, GLA_DV), F32),
                        pltpu.VMEM((nseq * GLA_HEADS, GLA_DV, GLA_DK), F32)],
        compiler_params=_cparams(48),
        name="gla",
    )(zn, zn, zn, zn, loga, ng, rmat)


DIL_UNROLL = 16
DIL_ROWS_PER_STEP = 4096


def _dil_kernel(q_ref, k_ref, v_ref, o_ref, st_ref, *, n_res, sub_len, chained):
    blk = DIL_BLOCK
    nb = sub_len // blk
    nk = 2 * blk
    qi = lax.broadcasted_iota(jnp.int32, (blk, nk), 0)
    km = lax.broadcasted_iota(jnp.int32, (blk, nk), 1)
    bias_window = jnp.where((km >= qi) & (km <= qi + blk), 0.0, NEG_BIG).astype(F32)
    bias_lead = jnp.where(km <= qi, 0.0, NEG_BIG).astype(F32)
    bias_trail = jnp.where((km >= blk) & (km <= qi + blk), 0.0, NEG_BIG).astype(F32)
    ones = jnp.ones((nk, DIL_HD), BF16)
    lane = lax.broadcasted_iota(jnp.int32, (blk, LANES), 1)
    lanes_per_head = LANES // DIL_HEADS
    max_lane = (lane & (lanes_per_head // 2)) == 0

    def unit(res, q0, k0, bias):
        stats = jnp.zeros((blk, LANES), F32)
        for h in range(DIL_HEADS):
            sl = slice(h * DIL_HD, (h + 1) * DIL_HD)
            q = q_ref[res, pl.ds(q0, blk), sl]
            kc = k_ref[res, pl.ds(k0, nk), sl]
            vc = jnp.concatenate([v_ref[res, pl.ds(k0, nk), sl], ones], axis=1)
            s = _dot_nt(q, kc) + bias
            mx = jnp.max(s, axis=-1, keepdims=True)
            p = jnp.exp(s - mx)
            ov = _dot(p.astype(BF16), vc)
            o_ref[res, pl.ds(q0, blk), sl] = ov[:, :DIL_HD].astype(BF16)
            stats = jnp.where(lane >= h * lanes_per_head, jnp.where(max_lane, mx, ov[:, DIL_HD:]), stats)
        st_ref[res, pl.ds(q0, blk), :] = stats

    assert (n_res * nb) % DIL_UNROLL == 0

    def body(g, carry):
        for u in range(DIL_UNROLL):
            uid = g * DIL_UNROLL + u
            res = uid >> (nb.bit_length() - 1)
            i = uid & (nb - 1)
            q0 = pl.multiple_of(i * blk, blk)
            if chained:
                k0 = pl.multiple_of(jnp.maximum(q0 - blk, 0), blk)
                bias = jnp.where(i == 0, bias_lead, bias_window)
            else:
                k0 = pl.multiple_of(jnp.minimum(q0, sub_len - nk), blk)
                bias = jnp.where(i == nb - 1, bias_trail, bias_lead)
            unit(res, q0, k0, bias)
        return carry

    lax.fori_loop(0, (n_res * nb) // DIL_UNROLL, body, 0)


def _dil_group(zq, chained, name):
    _, nsub, sub_len, _ = zq.shape
    n_res = DIL_ROWS_PER_STEP // sub_len
    assert sub_len >= 2 * DIL_BLOCK and nsub % n_res == 0

    def zspec(part):
        return pl.BlockSpec((None, n_res, sub_len, DIL_GW), lambda s: (part, s, 0, 0))

    return pl.pallas_call(
        functools.partial(_dil_kernel, n_res=n_res, sub_len=sub_len, chained=chained),
        grid=(nsub // n_res,),
        in_specs=[zspec(0), zspec(1), zspec(2)],
        out_specs=[pl.BlockSpec((n_res, sub_len, DIL_GW), lambda s: (s, 0, 0)),
                   pl.BlockSpec((n_res, sub_len, LANES), lambda s: (s, 0, 0))],
        out_shape=[jax.ShapeDtypeStruct((nsub, sub_len, DIL_GW), BF16),
                   jax.ShapeDtypeStruct((nsub, sub_len, LANES), F32)],
        compiler_params=_cparams(48),
        name=name,
    )(zq, zq, zq)


MERGE_TM = 512


def _shifted(prev_rows, cur, tm):
    full = jnp.concatenate([prev_rows, cur], axis=0)
    d1 = pltpu.roll(full, 1, 0)[SUBLANES:SUBLANES + tm]
    d2 = pltpu.roll(full, 2, 0)[SUBLANES:SUBLANES + tm]
    return d1, d2


def _merge_kernel(x_ref, scb_ref, scc_ref, scx_ref, ygla_ref, o0_ref, o1_ref, o2_ref,
                  l0_ref, l1_ref, l2_ref, gpre_ref, wg_ref, bg_ref, wbr_ref, wmix_ref,
                  gpost_ref, cw_ref, out_ref, halo_ref, oi1_ref, oi2_ref, li1_ref, li2_ref):
    tm = MERGE_TM

    @pl.when(pl.program_id(1) == 0)
    def _():
        halo_ref[...] = jnp.zeros(halo_ref.shape, F32)

    x = x_ref[...]
    h = _rms(x, gpre_ref[...]).astype(BF16)

    u = scc_ref[...].astype(F32) * scx_ref[...].astype(F32)
    u1, u2 = _shifted(halo_ref[...], u, tm)
    halo_ref[...] = u[tm - SUBLANES:, :]
    cw = cw_ref[...]
    y_sc = scb_ref[...].astype(F32) * (cw[0:1, :] * u2 + cw[1:2, :] * u1 + cw[2:3, :] * u)

    for (o_ref, l_ref, oi_ref, li_ref), (_, dil) in zip(
            ((o1_ref, l1_ref, oi1_ref, li1_ref), (o2_ref, l2_ref, oi2_ref, li2_ref)), DIL_PATTERNS[1:]):
        n = tm // dil
        for r in range(dil):
            o_r = o_ref[r].astype(F32)
            for hh in range(DIL_HEADS):
                oi_ref[hh, pl.ds(r, n, stride=dil), :] = o_r[:, hh * DIL_HD:(hh + 1) * DIL_HD]
            li_ref[pl.ds(r, n, stride=dil), :] = l_ref[r]

    stats = (l0_ref[0], li1_ref[...], li2_ref[...])
    lanes_per_head = LANES // DIL_HEADS
    parts = []
    for hh in range(DIL_HEADS):
        sl = slice(hh * DIL_HD, (hh + 1) * DIL_HD)
        lm = hh * lanes_per_head
        ld = lm + lanes_per_head // 2
        ma, mb, mc = (v[:, lm:lm + 1] for v in stats)
        da, db, dc = (v[:, ld:ld + 1] for v in stats)
        m = jnp.maximum(jnp.maximum(ma, mb), mc)
        ea, eb, ec = jnp.exp(ma - m), jnp.exp(mb - m), jnp.exp(mc - m)
        inv = 1.0 / (ea * da + eb * db + ec * dc)
        parts.append((ea * inv) * o0_ref[0, :, sl].astype(F32)
                     + (eb * inv) * oi1_ref[hh] + (ec * inv) * oi2_ref[hh])
    y_dil = jnp.concatenate(parts, axis=1)

    merged = jnp.zeros((tm, D_MODEL), F32)
    branches = (ygla_ref[...], y_sc.astype(BF16), y_dil.astype(BF16))
    for g, br in enumerate(branches):
        cs = slice(g * D_MODEL, (g + 1) * D_MODEL)
        gate = _sigmoid(_dot(h, wg_ref[:, cs]) + bg_ref[:, cs])
        merged = merged + gate * _dot(br, wbr_ref[g])
    mix = _dot(merged.astype(BF16), wmix_ref[...])
    out_ref[...] = x + _rms(mix, gpost_ref[...])


def _merge(x3, zn, ygla, o_list, l_list, layer, gpre, wg, bg, wbr, wmix, gpost, cw):
    batch, seq, d = x3.shape
    tm = MERGE_TM

    def rows(width, blk_idx=0):
        return pl.BlockSpec((None, tm, width), lambda b, i: (b, i, blk_idx))

    def grouped(dil, width):
        return pl.BlockSpec((None, dil, tm // dil, width), lambda b, i: (b, 0, i, 0))

    dils = [dil for _, dil in DIL_PATTERNS]
    return pl.pallas_call(
        _merge_kernel,
        grid=(batch, seq // tm),
        in_specs=[rows(d),
                  rows(BRANCH_WIDTH, ZB_SCB), rows(BRANCH_WIDTH, ZB_SCC), rows(BRANCH_WIDTH, ZB_SCX),
                  rows(GLA_V)]
                 + [grouped(dil, DIL_GW) for dil in dils]
                 + [grouped(dil, LANES) for dil in dils]
                 + [_const_spec((1, d)),
                    _layer_spec((d, 3 * d), layer),
                    _const_spec((1, 3 * d)),
                    _layer_spec((3, BRANCH_WIDTH, d), layer),
                    _layer_spec((d, d), layer),
                    _const_spec((1, d)),
                    _const_spec((CONV_K, BRANCH_WIDTH))],
        out_specs=rows(d),
        out_shape=jax.ShapeDtypeStruct((batch, seq, d), F32),
        scratch_shapes=[pltpu.VMEM((SUBLANES, BRANCH_WIDTH), F32),
                        pltpu.VMEM((DIL_HEADS, tm, DIL_HD), F32), pltpu.VMEM((DIL_HEADS, tm, DIL_HD), F32),
                        pltpu.VMEM((tm, LANES), F32), pltpu.VMEM((tm, LANES), F32)],
        compiler_params=_cparams(48, ("arbitrary", "arbitrary")),
        name="merge",
    )(x3, zn, zn, zn, ygla, *o_list, *l_list, gpre, wg, bg, wbr, wmix, gpost, cw)


FFN_TM = 512
FFN_CHUNKS = ((0, 1536), (1536, D_FF))


def _ffn_kernel(x_ref, g1_ref, wg_ref, wu_ref, cw_ref, cb_ref, wd_ref, g2_ref, out_ref, halo_ref):
    tm = FFN_TM

    @pl.when(pl.program_id(1) == 0)
    def _():
        halo_ref[...] = jnp.zeros(halo_ref.shape, F32)

    x = x_ref[...]
    h = _rms(x, g1_ref[...]).astype(BF16)
    acc = jnp.zeros((tm, D_MODEL), F32)
    for c0, c1 in FFN_CHUNKS:
        a = _dot(h, wg_ref[:, c0:c1])
        a1, a2 = _shifted(halo_ref[:, c0:c1], a, tm)
        halo_ref[:, c0:c1] = a[tm - SUBLANES:, :]
        cw = cw_ref[:, c0:c1]
        gt = cw[0:1, :] * a2 + cw[1:2, :] * a1 + cw[2:3, :] * a + cb_ref[:, c0:c1]
        ge = 0.5 * gt * (1.0 + jnp.tanh(0.7978845608028654 * (gt + 0.044715 * (gt * gt * gt))))
        up = _dot(h, wu_ref[:, c0:c1])
        acc = acc + _dot((ge * up).astype(BF16), wd_ref[c0:c1, :])
    out_ref[...] = x + _rms(acc, g2_ref[...])


def _ffn(x3, layer, g1, wg, wu, cw, cb, wd, g2):
    batch, seq, d = x3.shape
    tm = FFN_TM
    xspec = pl.BlockSpec((None, tm, d), lambda b, i: (b, i, 0))
    return pl.pallas_call(
        _ffn_kernel,
        grid=(batch, seq // tm),
        in_specs=[xspec,
                  _const_spec((1, d)),
                  _layer_spec((d, D_FF), layer),
                  _layer_spec((d, D_FF), layer),
                  _const_spec((CONV_K, D_FF)),
                  _const_spec((1, D_FF)),
                  _layer_spec((D_FF, d), layer),
                  _const_spec((1, d))],
        out_specs=xspec,
        out_shape=jax.ShapeDtypeStruct((batch, seq, d), F32),
        scratch_shapes=[pltpu.VMEM((SUBLANES, D_FF), F32)],
        compiler_params=_cparams(52, ("arbitrary", "arbitrary")),
        name="ffn",
    )(x3, g1, wg, wu, cw, cb, wd, g2)


def kernel(x, positions, w_in, w_alpha_up, b_alpha, gla_norm_g, sc_conv_w, w_gate, b_gate, w_branch,
           w_mix_out, pre_mix_g, post_mix_g, pre_ffn_g, post_ffn_g, w_ff_gate, w_ff_up, ff_conv_w,
           ff_conv_b, w_ff_down):
    batch, seq, d = x.shape
    depth = w_in.shape[0]
    tables = _rope_tables(positions)
    w_perm = _permute_w_in(w_in)
    wg_b, wbr_b, wmix_b = _to_bf16(w_gate), _to_bf16(w_branch), _to_bf16(w_mix_out)
    wfg_b, wfu_b, wfd_b = _to_bf16(w_ff_gate), _to_bf16(w_ff_up), _to_bf16(w_ff_down)
    rmat = jnp.asarray(_band_matrices(), BF16)

    for l in range(depth):
        wup = jnp.concatenate(
            [w_alpha_up[l], jnp.zeros((ALOW_PAD - GLA_RANK, GLA_QK), F32)], axis=0).astype(BF16)
        zn, z0, z1, z2, loga = _inproj(x, pre_mix_g[l].reshape(1, d), w_perm, l, tables,
                                       wup, b_alpha[l].reshape(1, GLA_QK))
        ygla = _gla(zn, loga, gla_norm_g[l].reshape(1, GLA_DV), rmat)
        o_list, l_list = [], []
        for zq, (_, dil) in zip((z0, z1, z2), DIL_PATTERNS):
            sub_len = seq // dil
            if sub_len == DIL_BLOCK:
                o, st = _dil_group(zq.reshape(3, batch, seq, DIL_GW), False, f"dil_attn_d{dil}")
            else:
                o, st = _dil_group(zq.reshape(3, batch * dil, sub_len, DIL_GW), True, f"dil_attn_d{dil}")
            o_list.append(o.reshape(batch, dil, sub_len, DIL_GW))
            l_list.append(st.reshape(batch, dil, sub_len, LANES))
        x = _merge(x, zn, ygla, o_list, l_list, l, pre_mix_g[l].reshape(1, d),
                   wg_b, b_gate[l].reshape(1, 3 * d), wbr_b, wmix_b, post_mix_g[l].reshape(1, d), sc_conv_w[l])
        x = _ffn(x, l, pre_ffn_g[l].reshape(1, d), wfg_b, wfu_b, ff_conv_w[l], ff_conv_b[l].reshape(1, D_FF),
                 wfd_b, post_ffn_g[l].reshape(1, d))
    return x
```

```python
import functools

import jax
import jax.numpy as jnp
import numpy as np
from jax import lax
from jax.experimental import pallas as pl
from jax.experimental.pallas import tpu as pltpu

F32 = jnp.float32
BF16 = jnp.bfloat16

D_MODEL = 1024
EPS = 1e-6
BRANCH_WIDTH = 512
GLA_HEADS = 4
GLA_DV = 128
GLA_DK = 64
GLA_RANK = 16
GLA_TAU = 16.0
GLA_CHUNK = 16
GLA_QK = GLA_HEADS * GLA_DK
GLA_V = GLA_HEADS * GLA_DV
CONV_K = 3
DIL_PATTERNS = ((128, 1), (512, 4), (2048, 16))
DIL_GROUPS = len(DIL_PATTERNS)
DIL_HEADS = 4
DIL_HD = 128
DIL_GW = DIL_HEADS * DIL_HD
DIL_BLOCK = 128
ROPE_THETA = 10000.0
D_FF = 2816

LANES = 128
SUBLANES = 8
MIB = 1024 * 1024

ALOW_PAD = LANES
ZN_COLS = 2 * GLA_V + 3 * BRANCH_WIDTH + 2 * GLA_QK
ZB_GV, ZB_GO, ZB_SCB, ZB_SCC, ZB_SCX = 0, 1, 2, 3, 4
ZB_GQ, ZB_GK = 10, 11
ZG_COLS = 3 * DIL_GW
W_ALOW = DIL_GROUPS * ZG_COLS + ZN_COLS
W_COLS = W_ALOW + ALOW_PAD

NEG_BIG = -1e30


def _cparams(vmem_mib, sem=None):
    return pltpu.CompilerParams(dimension_semantics=sem, vmem_limit_bytes=int(vmem_mib * MIB))


def _const_spec(shape):
    nd = len(shape)
    return pl.BlockSpec(shape, lambda *_: (0,) * nd, pipeline_mode=pl.Buffered(1))


def _layer_spec(shape, layer):
    nd = len(shape)
    return pl.BlockSpec((None,) + tuple(shape), lambda *_: (layer,) + (0,) * nd, pipeline_mode=pl.Buffered(1))


def _rms(xf, g):
    r = lax.rsqrt(jnp.mean(xf * xf, axis=-1, keepdims=True) + EPS)
    return xf * r * g


def _sigmoid(v):
    return 1.0 / (1.0 + jnp.exp(-v))


def _dot(a, b):
    return jnp.dot(a, b, preferred_element_type=F32)


def _dot_nt(a, b):
    return lax.dot_general(a, b, (((1,), (1,)), ((), ())), preferred_element_type=F32)


def _dot_tn(a, b):
    return lax.dot_general(a, b, (((0,), (0,)), ((), ())), preferred_element_type=F32)


def _rope_kernel(pos_ref, inv_ref, sgn_ref, c0_ref, s0_ref, c1_ref, s1_ref, c2_ref, s2_ref, *, seq):
    half = seq // 2
    lane = lax.broadcasted_iota(jnp.int32, (half, DIL_HD), 1)
    low = lane < DIL_HD // 2
    pos = jnp.where(low, pos_ref[0:half, :], pos_ref[half:seq, :]).astype(F32)
    ang = pos * inv_ref[...]
    cs, sn = jnp.cos(ang), jnp.sin(ang)
    cs_sw, sn_sw = pltpu.roll(cs, DIL_HD // 2, 1), pltpu.roll(sn, DIL_HD // 2, 1)
    c0_ref[0, 0:half, :] = jnp.where(low, cs, cs_sw)
    c0_ref[0, half:seq, :] = jnp.where(low, cs_sw, cs)
    s0_ref[0, 0:half, :] = jnp.where(low, sn, sn_sw) * sgn_ref[...]
    s0_ref[0, half:seq, :] = jnp.where(low, sn_sw, sn) * sgn_ref[...]
    for (c_ref, s_ref), (_, dil) in zip(((c1_ref, s1_ref), (c2_ref, s2_ref)), DIL_PATTERNS[1:]):
        n = seq // dil
        for r in range(dil):
            c_ref[r] = c0_ref[0, pl.ds(r, n, stride=dil), :]
            s_ref[r] = s0_ref[0, pl.ds(r, n, stride=dil), :]


def _rope_tables(positions):
    batch, seq = positions.shape
    inv = ROPE_THETA ** (-jnp.arange(0, DIL_HD, 2, dtype=F32) / DIL_HD)
    inv2 = jnp.concatenate([inv, inv]).reshape(1, DIL_HD)
    half = DIL_HD // 2
    sgn = jnp.concatenate([-jnp.ones((half,), F32), jnp.ones((half,), F32)]).reshape(1, DIL_HD)
    shapes, specs = [], []
    for _, dil in DIL_PATTERNS:
        shp = (batch, dil, seq // dil, DIL_HD)
        for _ in range(2):
            shapes.append(jax.ShapeDtypeStruct(shp, F32))
            specs.append(pl.BlockSpec((None,) + shp[1:], lambda b: (b, 0, 0, 0)))
    outs = pl.pallas_call(
        functools.partial(_rope_kernel, seq=seq),
        grid=(batch,),
        in_specs=[pl.BlockSpec((None, seq, 1), lambda b: (b, 0, 0)),
                  pl.BlockSpec((1, DIL_HD), lambda b: (0, 0)),
                  pl.BlockSpec((1, DIL_HD), lambda b: (0, 0))],
        out_specs=specs,
        out_shape=shapes,
        compiler_params=_cparams(32),
        name="rope_tables",
    )(positions.reshape(batch, seq, 1), inv2, sgn)
    return [(outs[2 * g], outs[2 * g + 1]) for g in range(DIL_GROUPS)]


CAST_ROWS = 512


def _cast_kernel(w_ref, o_ref):
    o_ref[...] = w_ref[...].astype(BF16)


def _to_bf16(w):
    rows, cols = w.size // w.shape[-1], w.shape[-1]
    out = pl.pallas_call(
        _cast_kernel,
        grid=(rows // CAST_ROWS,),
        in_specs=[pl.BlockSpec((CAST_ROWS, cols), lambda i: (i, 0))],
        out_specs=pl.BlockSpec((CAST_ROWS, cols), lambda i: (i, 0)),
        out_shape=jax.ShapeDtypeStruct((rows, cols), BF16),
        compiler_params=_cparams(32),
        name="weights_to_bf16",
    )(w.reshape(rows, cols))
    return out.reshape(w.shape)


_IN_SIZES = (GLA_QK, GLA_QK, GLA_V, GLA_V, GLA_RANK, BRANCH_WIDTH, BRANCH_WIDTH, BRANCH_WIDTH,
             DIL_GROUPS * DIL_GW, DIL_GROUPS * DIL_GW, DIL_GROUPS * DIL_GW)
(_O_GQ, _O_GK, _O_GV, _O_GO, _O_AL, _O_SCB, _O_SCC, _O_SCX, _O_DQ, _O_DK, _O_DV) = (
    sum(_IN_SIZES[:i]) for i in range(len(_IN_SIZES)))
_W_IN_SEGMENTS = tuple((o + gi * DIL_GW, DIL_GW) for gi in range(DIL_GROUPS) for o in (_O_DQ, _O_DK, _O_DV)) + (
    (_O_GV, 2 * GLA_V), (_O_SCB, 3 * BRANCH_WIDTH), (_O_GQ, 2 * GLA_QK), (_O_AL, GLA_RANK))
W_PREP_LANES = 128


def _w_in_kernel(w_ref, o_ref):
    dst = 0
    for src, width in _W_IN_SEGMENTS:
        o_ref[dst:dst + width, :] = w_ref[src:src + width, :].astype(BF16)
        dst += width
    o_ref[dst:W_COLS, :] = jnp.zeros((W_COLS - dst, o_ref.shape[1]), BF16)


def _permute_w_in(w_in):
    depth, d, cols = w_in.shape
    return pl.pallas_call(
        _w_in_kernel,
        grid=(depth, d // W_PREP_LANES),
        in_specs=[pl.BlockSpec((None, cols, W_PREP_LANES), lambda l, i: (l, 0, i))],
        out_specs=pl.BlockSpec((None, W_COLS, W_PREP_LANES), lambda l, i: (l, 0, i)),
        out_shape=jax.ShapeDtypeStruct((depth, W_COLS, d), BF16),
        compiler_params=_cparams(32),
        name="w_in_permute",
    )(jnp.swapaxes(w_in, 1, 2))


INPROJ_TM = 512
INPROJ_CH = 1024


def _inproj_kernel(x_ref, g_ref, w_ref, c0_ref, s0_ref, c1_ref, s1_ref, c2_ref, s2_ref, wup_ref, ba_ref,
                   zn_ref, z0_ref, z1_ref, z2_ref, la_ref, h_ref):
    tm = INPROJ_TM
    nlt = D_MODEL // LANES
    hf = _rms(x_ref[...], g_ref[...])
    for j in range(nlt):
        h_ref[j] = hf[:, j * LANES:(j + 1) * LANES]
    h = hf.astype(BF16)

    groups = ((z0_ref, c0_ref, s0_ref), (z1_ref, c1_ref, s1_ref), (z2_ref, c2_ref, s2_ref))
    for gi, (z_ref, c_ref, s_ref) in enumerate(groups):
        dil = DIL_PATTERNS[gi][1]
        n = tm // dil
        if dil == 1:
            hp = h
        else:
            hp = jnp.concatenate(
                [jnp.concatenate([h_ref[j, pl.ds(r, n, stride=dil), :] for j in range(nlt)], axis=1)
                 for r in range(dil)], axis=0).astype(BF16)
        cs = jnp.concatenate([c_ref[r] for r in range(dil)], axis=0)
        sn = jnp.concatenate([s_ref[r] for r in range(dil)], axis=0)
        for part in range(3):
            w0 = gi * ZG_COLS + part * DIL_GW
            res = _dot_nt(hp, w_ref[w0:w0 + DIL_GW, :])
            if part < 2:
                sc = DIL_HD ** -0.5 if part == 0 else 1.0
                heads = []
                for hh in range(DIL_HEADS):
                    xh = res[:, hh * DIL_HD:(hh + 1) * DIL_HD]
                    heads.append(xh * (cs * sc) + pltpu.roll(xh, DIL_HD // 2, 1) * (sn * sc))
                res = jnp.concatenate(heads, axis=1)
            resb = res.astype(BF16)
            for r in range(dil):
                z_ref[part, r] = resb[r * n:(r + 1) * n, :]

    wn = DIL_GROUPS * ZG_COLS
    for c0 in range(0, ZN_COLS, INPROJ_CH):
        c1 = min(c0 + INPROJ_CH, ZN_COLS)
        zn_ref[:, c0:c1] = _dot_nt(h, w_ref[wn + c0:wn + c1, :]).astype(BF16)

    a_low = _dot_nt(h, w_ref[W_ALOW:W_COLS, :]).astype(BF16)
    xa = _dot(a_low, wup_ref[...]) + ba_ref[...]
    la_ref[...] = (jnp.minimum(xa, 0.0) - jnp.log(1.0 + jnp.exp(-jnp.abs(xa)))) * (1.0 / GLA_TAU)


def _inproj(x3, g, w_all, layer, tables, wup, ba):
    batch, seq, d = x3.shape
    tm = INPROJ_TM
    out_shapes = [jax.ShapeDtypeStruct((batch, seq, ZN_COLS), BF16)]
    out_specs = [pl.BlockSpec((None, tm, ZN_COLS), lambda b, i: (b, i, 0))]
    tab_specs, tab_args = [], []
    for (_, dil), (cos, sin) in zip(DIL_PATTERNS, tables):
        out_shapes.append(jax.ShapeDtypeStruct((3, batch, dil, seq // dil, DIL_GW), BF16))
        out_specs.append(pl.BlockSpec((3, None, dil, tm // dil, DIL_GW), lambda b, i: (0, b, 0, i, 0)))
        tab_specs += [pl.BlockSpec((None, dil, tm // dil, DIL_HD), lambda b, i: (b, 0, i, 0))] * 2
        tab_args += [cos, sin]
    out_shapes.append(jax.ShapeDtypeStruct((batch, seq, GLA_QK), F32))
    out_specs.append(pl.BlockSpec((None, tm, GLA_QK), lambda b, i: (b, i, 0)))
    return pl.pallas_call(
        _inproj_kernel,
        grid=(batch, seq // tm),
        in_specs=[pl.BlockSpec((None, tm, d), lambda b, i: (b, i, 0)),
                  _const_spec((1, d)),
                  pl.BlockSpec((None, W_COLS, d), lambda b, i: (layer, 0, 0),
                               pipeline_mode=pl.Buffered(1))] + tab_specs
                 + [_const_spec((ALOW_PAD, GLA_QK)), _const_spec((1, GLA_QK))],
        out_specs=out_specs,
        out_shape=out_shapes,
        scratch_shapes=[pltpu.VMEM((d // LANES, tm, LANES), F32)],
        compiler_params=_cparams(52),
        name="inproj",
    )(x3, g, w_all, *tab_args, wup, ba)


GLA_TS = 128
GLA_NCH = GLA_TS // GLA_CHUNK
GLA_FAST_SPAN = 60.0
GLA_FAST_TS = 256
GLA_FAST_UNROLL = 2
GLA_SEQ_PER_STEP = 2


def _band_col(h, r):
    base = (h // 2) * LANES
    if h % 2 == 0:
        return base + (LANES - r) % LANES
    return base + 2 * GLA_CHUNK - r


def _band_matrices():
    m = np.zeros((GLA_CHUNK, GLA_QK, 2 * LANES), np.float32)
    for r in range(GLA_CHUNK):
        for h in range(GLA_HEADS):
            m[r, h * GLA_DK:(h + 1) * GLA_DK, _band_col(h, r)] = 1.0
    return m


def _gla_kernel(q_ref, k_ref, v_ref, go_ref, la_ref, ng_ref, rmat_ref, y_ref, s_ref, st_ref, *, seq):
    ts, c, nch, nseq = GLA_TS, GLA_CHUNK, GLA_NCH, GLA_SEQ_PER_STEP
    s_ref[...] = jnp.zeros(s_ref.shape, F32)
    st_ref[...] = jnp.zeros(st_ref.shape, F32)

    def span_tile(ti, worst):
        r0 = pl.multiple_of(ti * ts, ts)
        for b in range(nseq):
            worst = jnp.maximum(worst, -jnp.sum(la_ref[b, pl.ds(r0, ts), :], axis=0, keepdims=True))
        return worst

    worst = lax.fori_loop(0, seq // ts, span_tile, jnp.zeros((1, GLA_QK), F32))
    decay_span = jnp.max(worst)

    row = lax.broadcasted_iota(jnp.int32, (ts, ts), 0)
    col = lax.broadcasted_iota(jnp.int32, (ts, ts), 1)
    same = (row >> 4) == (col >> 4)
    tri = jnp.where(same, jnp.where(col <= row, 1.0, 0.0), 0.0).astype(F32)
    blk = jnp.where(same, 1.0, 0.0).astype(F32)
    rowmod = lax.broadcasted_iota(jnp.int32, (ts, GLA_QK), 0) & (c - 1)
    keep_even = (col == 0) | (col > LANES - c)
    keep_odd = (col > c) & (col <= 2 * c)
    chunk_of_col = lax.broadcasted_iota(jnp.int32, (GLA_DK, ts), 1) >> 4
    q8_row = lax.broadcasted_iota(jnp.int32, (ts, nch * GLA_DK), 0) >> 4
    q8_col = lax.broadcasted_iota(jnp.int32, (ts, nch * GLA_DK), 1) >> 6

    def finish(o, b, r0, h, rows):
        on = o * lax.rsqrt(jnp.mean(o * o, axis=-1, keepdims=True) + EPS) * ng_ref[...]
        g = go_ref[b, pl.ds(r0, rows), h * GLA_DV:(h + 1) * GLA_DV].astype(F32)
        y_ref[b, pl.ds(r0, rows), h * GLA_DV:(h + 1) * GLA_DV] = (on * g * _sigmoid(g)).astype(BF16)

    def safe_tile(b, r0):
        q = q_ref[b, pl.ds(r0, ts), :].astype(F32) * (GLA_DK ** -0.5)
        k = k_ref[b, pl.ds(r0, ts), :].astype(F32)
        loga = la_ref[b, pl.ds(r0, ts), :]
        lc = jnp.dot(tri, loga, precision=lax.Precision.HIGHEST, preferred_element_type=F32)
        lend = jnp.dot(blk, loga, precision=lax.Precision.HIGHEST, preferred_element_type=F32)
        q_in = q * jnp.exp(lc)
        k_in = k * jnp.exp(lend - lc)

        band = _dot((q * k).astype(BF16), rmat_ref[0])
        rel = jnp.zeros_like(loga)
        for r in range(1, c):
            rel = rel + (loga if r == 1 else pltpu.roll(loga, r - 1, 0))
            term = jnp.where(rowmod >= r, q * pltpu.roll(k, r, 0) * jnp.exp(rel), 0.0)
            band = band + _dot(term.astype(BF16), rmat_ref[r])

        kin_t = k_in.T
        lend_t = lend.T
        for h in range(GLA_HEADS):
            vh = v_ref[b, pl.ds(r0, ts), h * GLA_DV:(h + 1) * GLA_DV]
            bt = band[:, (h // 2) * LANES:(h // 2 + 1) * LANES]
            if h % 2 == 0:
                p = pltpu.roll(jnp.where(keep_even, bt, 0.0), 0, 1, stride=1, stride_axis=0)
            else:
                p = pltpu.roll(jnp.where(keep_odd, bt, 0.0), LANES - 2 * c, 1, stride=1, stride_axis=0)
            o = _dot(p.astype(BF16), vh)

            kt = kin_t[h * GLA_DK:(h + 1) * GLA_DK, :]
            kst = jnp.concatenate(
                [jnp.where(chunk_of_col == n, kt, 0.0) for n in range(nch)], axis=0).astype(BF16)
            u = _dot(kst, vh)
            lt = lend_t[h * GLA_DK:(h + 1) * GLA_DK, :]
            st = s_ref[b * GLA_HEADS + h]
            states = []
            for n in range(nch):
                states.append(st)
                st = st * jnp.exp(lt[:, n * c:n * c + 1]) + u[n * GLA_DK:(n + 1) * GLA_DK, :]
            s_ref[b * GLA_HEADS + h] = st
            sst = jnp.concatenate(states, axis=0).astype(BF16)
            qh = q_in[:, h * GLA_DK:(h + 1) * GLA_DK]
            q8 = jnp.concatenate([qh] * nch, axis=1)
            qx = jnp.where(q8_row == q8_col, q8, 0.0).astype(BF16)
            finish(o + _dot(qx, sst), b, r0, h, ts)

    tf = GLA_FAST_TS
    causal = (lax.broadcasted_iota(jnp.int32, (tf, tf), 1) <= lax.broadcasted_iota(jnp.int32, (tf, tf), 0))
    tri_full = jnp.where(causal, 1.0, 0.0).astype(BF16)

    def fast_tile(b, r0):
        loga = la_ref[b, pl.ds(r0, tf), :]
        hi = loga.astype(BF16)
        lo = (loga - hi.astype(F32)).astype(BF16)
        cum = _dot(tri_full, hi) + _dot(tri_full, lo)
        lend = cum[tf - 1:tf, :]
        dec = jnp.exp(lend)
        q = q_ref[b, pl.ds(r0, tf), :].astype(F32) * (GLA_DK ** -0.5)
        k = k_ref[b, pl.ds(r0, tf), :].astype(F32)
        qt = (q * jnp.exp(cum)).astype(BF16)
        kt = k * jnp.exp(-cum)
        ks = (kt * dec).astype(BF16)
        kt = kt.astype(BF16)
        for h in range(GLA_HEADS):
            ds = slice(h * GLA_DK, (h + 1) * GLA_DK)
            vh = v_ref[b, pl.ds(r0, tf), h * GLA_DV:(h + 1) * GLA_DV]
            p = jnp.where(causal, _dot_nt(qt[:, ds], kt[:, ds]), 0.0).astype(BF16)
            st = st_ref[b * GLA_HEADS + h]
            o = _dot(p, vh) + _dot_nt(qt[:, ds], st.astype(BF16))
            st_ref[b * GLA_HEADS + h] = st * dec[:, ds] + _dot_tn(vh, ks[:, ds])
            finish(o, b, r0, h, tf)

    def run(tile_fn, rows, unroll):
        def body(ti, carry):
            r0 = pl.multiple_of(ti * rows, rows)
            for b in range(nseq):
                tile_fn(b, r0)
            return carry
        lax.fori_loop(0, seq // rows, body, 0, unroll=unroll)

    fast_ok = decay_span * (tf // ts) <= GLA_FAST_SPAN

    @pl.when(fast_ok)
    def _():
        run(fast_tile, tf, GLA_FAST_UNROLL)

    @pl.when(jnp.logical_not(fast_ok))
    def _():
        run(safe_tile, ts, 1)


def _gla(zn, loga, ng, rmat):
    batch, seq, _ = zn.shape
    nseq = GLA_SEQ_PER_STEP
    assert batch % nseq == 0

    def zspec(width, blk_idx):
        return pl.BlockSpec((nseq, seq, width), lambda b: (b, 0, blk_idx))

    return pl.pallas_call(
        functools.partial(_gla_kernel, seq=seq),
        grid=(batch // nseq,),
        in_specs=[zspec(GLA_QK, ZB_GQ), zspec(GLA_QK, ZB_GK), zspec(GLA_V, ZB_GV),
                  zspec(GLA_V, ZB_GO), zspec(GLA_QK, 0),
                  _const_spec((1, GLA_DV)),
                  _const_spec((GLA_CHUNK, GLA_QK, 2 * LANES))],
        out_specs=pl.BlockSpec((nseq, seq, GLA_V), lambda b: (b, 0, 0)),
        out_shape=jax.ShapeDtypeStruct((batch, seq, GLA_V), BF16),
        scratch_shapes=[pltpu.VMEM((nseq * GLA_HEADS, GLA_DK, GLA_DV), F32),
                        pltpu.VMEM((nseq * GLA_HEADS, GLA_DV, GLA_DK), F32)],
        compiler_params=_cparams(48),
        name="gla",
    )(zn, zn, zn, zn, loga, ng, rmat)


DIL_UNROLL = 16
DIL_ROWS_PER_STEP = 4096


def _dil_kernel(q_ref, k_ref, v_ref, o_ref, st_ref, *, n_res, sub_len, chained):
    blk = DIL_BLOCK
    nb = sub_len // blk
    nk = 2 * blk
    qi = lax.broadcasted_iota(jnp.int32, (blk, nk), 0)
    km = lax.broadcasted_iota(jnp.int32, (blk, nk), 1)
    bias_window = jnp.where((km >= qi) & (km <= qi + blk), 0.0, NEG_BIG).astype(F32)
    bias_lead = jnp.where(km <= qi, 0.0, NEG_BIG).astype(F32)
    bias_trail = jnp.where((km >= blk) & (km <= qi + blk), 0.0, NEG_BIG).astype(F32)
    ones = jnp.ones((nk, DIL_HD), BF16)
    lane = lax.broadcasted_iota(jnp.int32, (blk, LANES), 1)
    lanes_per_head = LANES // DIL_HEADS
    max_lane = (lane & (lanes_per_head // 2)) == 0

    def unit(res, q0, k0, bias):
        stats = jnp.zeros((blk, LANES), F32)
        for h in range(DIL_HEADS):
            sl = slice(h * DIL_HD, (h + 1) * DIL_HD)
            q = q_ref[res, pl.ds(q0, blk), sl]
            kc = k_ref[res, pl.ds(k0, nk), sl]
            vc = jnp.concatenate([v_ref[res, pl.ds(k0, nk), sl], ones], axis=1)
            s = _dot_nt(q, kc) + bias
            mx = jnp.max(s, axis=-1, keepdims=True)
            p = jnp.exp(s - mx)
            ov = _dot(p.astype(BF16), vc)
            o_ref[res, pl.ds(q0, blk), sl] = ov[:, :DIL_HD].astype(BF16)
            stats = jnp.where(lane >= h * lanes_per_head, jnp.where(max_lane, mx, ov[:, DIL_HD:]), stats)
        st_ref[res, pl.ds(q0, blk), :] = stats

    assert (n_res * nb) % DIL_UNROLL == 0

    def body(g, carry):
        for u in range(DIL_UNROLL):
            uid = g * DIL_UNROLL + u
            res = uid >> (nb.bit_length() - 1)
            i = uid & (nb - 1)
            q0 = pl.multiple_of(i * blk, blk)
            if chained:
                k0 = pl.multiple_of(jnp.maximum(q0 - blk, 0), blk)
                bias = jnp.where(i == 0, bias_lead, bias_window)
            else:
                k0 = pl.multiple_of(jnp.minimum(q0, sub_len - nk), blk)
                bias = jnp.where(i == nb - 1, bias_trail, bias_lead)
            unit(res, q0, k0, bias)
        return carry

    lax.fori_loop(0, (n_res * nb) // DIL_UNROLL, body, 0)


def _dil_group(zq, chained, name):
    _, nsub, sub_len, _ = zq.shape
    n_res = DIL_ROWS_PER_STEP // sub_len
    assert sub_len >= 2 * DIL_BLOCK and nsub % n_res == 0

    def zspec(part):
        return pl.BlockSpec((None, n_res, sub_len, DIL_GW), lambda s: (part, s, 0, 0))

    return pl.pallas_call(
        functools.partial(_dil_kernel, n_res=n_res, sub_len=sub_len, chained=chained),
        grid=(nsub // n_res,),
        in_specs=[zspec(0), zspec(1), zspec(2)],
        out_specs=[pl.BlockSpec((n_res, sub_len, DIL_GW), lambda s: (s, 0, 0)),
                   pl.BlockSpec((n_res, sub_len, LANES), lambda s: (s, 0, 0))],
        out_shape=[jax.ShapeDtypeStruct((nsub, sub_len, DIL_GW), BF16),
                   jax.ShapeDtypeStruct((nsub, sub_len, LANES), F32)],
        compiler_params=_cparams(48),
        name=name,
    )(zq, zq, zq)


MERGE_TM = 512


def _shifted(prev_rows, cur, tm):
    full = jnp.concatenate([prev_rows, cur], axis=0)
    d1 = pltpu.roll(full, 1, 0)[SUBLANES:SUBLANES + tm]
    d2 = pltpu.roll(full, 2, 0)[SUBLANES:SUBLANES + tm]
    return d1, d2


def _merge_kernel(x_ref, scb_ref, scc_ref, scx_ref, ygla_ref, o0_ref, o1_ref, o2_ref,
                  l0_ref, l1_ref, l2_ref, gpre_ref, wg_ref, bg_ref, wbr_ref, wmix_ref,
                  gpost_ref, cw_ref, out_ref, halo_ref, oi1_ref, oi2_ref, li1_ref, li2_ref):
    tm = MERGE_TM

    @pl.when(pl.program_id(1) == 0)
    def _():
        halo_ref[...] = jnp.zeros(halo_ref.shape, F32)

    x = x_ref[...]
    h = _rms(x, gpre_ref[...]).astype(BF16)

    u = scc_ref[...].astype(F32) * scx_ref[...].astype(F32)
    u1, u2 = _shifted(halo_ref[...], u, tm)
    halo_ref[...] = u[tm - SUBLANES:, :]
    cw = cw_ref[...]
    y_sc = scb_ref[...].astype(F32) * (cw[0:1, :] * u2 + cw[1:2, :] * u1 + cw[2:3, :] * u)

    for (o_ref, l_ref, oi_ref, li_ref), (_, dil) in zip(
            ((o1_ref, l1_ref, oi1_ref, li1_ref), (o2_ref, l2_ref, oi2_ref, li2_ref)), DIL_PATTERNS[1:]):
        n = tm // dil
        for r in range(dil):
            o_r = o_ref[r].astype(F32)
            for hh in range(DIL_HEADS):
                oi_ref[hh, pl.ds(r, n, stride=dil), :] = o_r[:, hh * DIL_HD:(hh + 1) * DIL_HD]
            li_ref[pl.ds(r, n, stride=dil), :] = l_ref[r]

    stats = (l0_ref[0], li1_ref[...], li2_ref[...])
    lanes_per_head = LANES // DIL_HEADS
    parts = []
    for hh in range(DIL_HEADS):
        sl = slice(hh * DIL_HD, (hh + 1) * DIL_HD)
        lm = hh * lanes_per_head
        ld = lm + lanes_per_head // 2
        ma, mb, mc = (v[:, lm:lm + 1] for v in stats)
        da, db, dc = (v[:, ld:ld + 1] for v in stats)
        m = jnp.maximum(jnp.maximum(ma, mb), mc)
        ea, eb, ec = jnp.exp(ma - m), jnp.exp(mb - m), jnp.exp(mc - m)
        inv = 1.0 / (ea * da + eb * db + ec * dc)
        parts.append((ea * inv) * o0_ref[0, :, sl].astype(F32)
                     + (eb * inv) * oi1_ref[hh] + (ec * inv) * oi2_ref[hh])
    y_dil = jnp.concatenate(parts, axis=1)

    merged = jnp.zeros((tm, D_MODEL), F32)
    branches = (ygla_ref[...], y_sc.astype(BF16), y_dil.astype(BF16))
    for g, br in enumerate(branches):
        cs = slice(g * D_MODEL, (g + 1) * D_MODEL)
        gate = _sigmoid(_dot(h, wg_ref[:, cs]) + bg_ref[:, cs])
        merged = merged + gate * _dot(br, wbr_ref[g])
    mix = _dot(merged.astype(BF16), wmix_ref[...])
    out_ref[...] = x + _rms(mix, gpost_ref[...])


def _merge(x3, zn, ygla, o_list, l_list, layer, gpre, wg, bg, wbr, wmix, gpost, cw):
    batch, seq, d = x3.shape
    tm = MERGE_TM

    def rows(width, blk_idx=0):
        return pl.BlockSpec((None, tm, width), lambda b, i: (b, i, blk_idx))

    def grouped(dil, width):
        return pl.BlockSpec((None, dil, tm // dil, width), lambda b, i: (b, 0, i, 0))

    dils = [dil for _, dil in DIL_PATTERNS]
    return pl.pallas_call(
        _merge_kernel,
        grid=(batch, seq // tm),
        in_specs=[rows(d),
                  rows(BRANCH_WIDTH, ZB_SCB), rows(BRANCH_WIDTH, ZB_SCC), rows(BRANCH_WIDTH, ZB_SCX),
                  rows(GLA_V)]
                 + [grouped(dil, DIL_GW) for dil in dils]
                 + [grouped(dil, LANES) for dil in dils]
                 + [_const_spec((1, d)),
                    _layer_spec((d, 3 * d), layer),
                    _const_spec((1, 3 * d)),
                    _layer_spec((3, BRANCH_WIDTH, d), layer),
                    _layer_spec((d, d), layer),
                    _const_spec((1, d)),
                    _const_spec((CONV_K, BRANCH_WIDTH))],
        out_specs=rows(d),
        out_shape=jax.ShapeDtypeStruct((batch, seq, d), F32),
        scratch_shapes=[pltpu.VMEM((SUBLANES, BRANCH_WIDTH), F32),
                        pltpu.VMEM((DIL_HEADS, tm, DIL_HD), F32), pltpu.VMEM((DIL_HEADS, tm, DIL_HD), F32),
                        pltpu.VMEM((tm, LANES), F32), pltpu.VMEM((tm, LANES), F32)],
        compiler_params=_cparams(48, ("arbitrary", "arbitrary")),
        name="merge",
    )(x3, zn, zn, zn, ygla, *o_list, *l_list, gpre, wg, bg, wbr, wmix, gpost, cw)


FFN_TM = 512
FFN_CHUNKS = ((0, 1536), (1536, D_FF))


def _ffn_kernel(x_ref, g1_ref, wg_ref, wu_ref, cw_ref, cb_ref, wd_ref, g2_ref, out_ref, halo_ref):
    tm = FFN_TM

    @pl.when(pl.program_id(1) == 0)
    def _():
        halo_ref[...] = jnp.zeros(halo_ref.shape, F32)

    x = x_ref[...]
    h = _rms(x, g1_ref[...]).astype(BF16)
    acc = jnp.zeros((tm, D_MODEL), F32)
    for c0, c1 in FFN_CHUNKS:
        a = _dot(h, wg_ref[:, c0:c1])
        a1, a2 = _shifted(halo_ref[:, c0:c1], a, tm)
        halo_ref[:, c0:c1] = a[tm - SUBLANES:, :]
        cw = cw_ref[:, c0:c1]
        gt = cw[0:1, :] * a2 + cw[1:2, :] * a1 + cw[2:3, :] * a + cb_ref[:, c0:c1]
        ge = 0.5 * gt * (1.0 + jnp.tanh(0.7978845608028654 * (gt + 0.044715 * (gt * gt * gt))))
        up = _dot(h, wu_ref[:, c0:c1])
        acc = acc + _dot((ge * up).astype(BF16), wd_ref[c0:c1, :])
    out_ref[...] = x + _rms(acc, g2_ref[...])


def _ffn(x3, layer, g1, wg, wu, cw, cb, wd, g2):
    batch, seq, d = x3.shape
    tm = FFN_TM
    xspec = pl.BlockSpec((None, tm, d), lambda b, i: (b, i, 0))
    return pl.pallas_call(
        _ffn_kernel,
        grid=(batch, seq // tm),
        in_specs=[xspec,
                  _const_spec((1, d)),
                  _layer_spec((d, D_FF), layer),
                  _layer_spec((d, D_FF), layer),
                  _const_spec((CONV_K, D_FF)),
                  _const_spec((1, D_FF)),
                  _layer_spec((D_FF, d), layer),
                  _const_spec((1, d))],
        out_specs=xspec,
        out_shape=jax.ShapeDtypeStruct((batch, seq, d), F32),
        scratch_shapes=[pltpu.VMEM((SUBLANES, D_FF), F32)],
        compiler_params=_cparams(52, ("arbitrary", "arbitrary")),
        name="ffn",
    )(x3, g1, wg, wu, cw, cb, wd, g2)


def kernel(x, positions, w_in, w_alpha_up, b_alpha, gla_norm_g, sc_conv_w, w_gate, b_gate, w_branch,
           w_mix_out, pre_mix_g, post_mix_g, pre_ffn_g, post_ffn_g, w_ff_gate, w_ff_up, ff_conv_w,
           ff_conv_b, w_ff_down):
    batch, seq, d = x.shape
    depth = w_in.shape[0]
    tables = _rope_tables(positions)
    w_perm = _permute_w_in(w_in)
    wg_b, wbr_b, wmix_b = _to_bf16(w_gate), _to_bf16(w_branch), _to_bf16(w_mix_out)
    wfg_b, wfu_b, wfd_b = _to_bf16(w_ff_gate), _to_bf16(w_ff_up), _to_bf16(w_ff_down)
    rmat = jnp.asarray(_band_matrices(), BF16)

    for l in range(depth):
        wup = jnp.concatenate(
            [w_alpha_up[l], jnp.zeros((ALOW_PAD - GLA_RANK, GLA_QK), F32)], axis=0).astype(BF16)
        zn, z0, z1, z2, loga = _inproj(x, pre_mix_g[l].reshape(1, d), w_perm, l, tables,
                                       wup, b_alpha[l].reshape(1, GLA_QK))
        ygla = _gla(zn, loga, gla_norm_g[l].reshape(1, GLA_DV), rmat)
        o_list, l_list = [], []
        for zq, (_, dil) in zip((z0, z1, z2), DIL_PATTERNS):
            sub_len = seq // dil
            if sub_len == DIL_BLOCK:
                o, st = _dil_group(zq.reshape(3, batch, seq, DIL_GW), False, f"dil_attn_d{dil}")
            else:
                o, st = _dil_group(zq.reshape(3, batch * dil, sub_len, DIL_GW), True, f"dil_attn_d{dil}")
            o_list.append(o.reshape(batch, dil, sub_len, DIL_GW))
            l_list.append(st.reshape(batch, dil, sub_len, LANES))
        x = _merge(x, zn, ygla, o_list, l_list, l, pre_mix_g[l].reshape(1, d),
                   wg_b, b_gate[l].reshape(1, 3 * d), wbr_b, wmix_b, post_mix_g[l].reshape(1, d), sc_conv_w[l])
        x = _ffn(x, l, pre_ffn_g[l].reshape(1, d), wfg_b, wfu_b, ff_conv_w[l], ff_conv_b[l].reshape(1, D_FF),
                 wfd_b, post_ffn_g[l].reshape(1, d))
    return x
```
